```python
import jax
import jax.numpy as jnp
from jax import lax
import numpy as np

D_MODEL = 1024
BATCH = 8
SEQ = 2048
DEPTH = 2
DEC_BATCH = 128
DEC_SEQ = 1
PAST_LEN = 16384
PAGE_SIZE = 128

GLA_HEADS = 4
GLA_DK = D_MODEL // 8
GLA_DV = D_MODEL // 4
GLA_RANK = 16
GLA_GATE_NORM = 16.0
GLA_CHUNK = 32
SC_WIDTH = D_MODEL
SC_CONV_W = 3
RG_WIDTH = D_MODEL
RG_BLOCKS = 8
RG_BLOCK = RG_WIDTH // RG_BLOCKS
RG_CONV_W = 4
RG_C = 8.0
N_BRANCH = 3
RMS_EPS = 1e-6

IN_WIDTHS = (
    GLA_HEADS * GLA_DK,
    GLA_HEADS * GLA_DK,
    GLA_HEADS * GLA_DV,
    GLA_HEADS * GLA_DV,
    GLA_RANK,
    SC_WIDTH,
    SC_WIDTH,
    SC_WIDTH,
    SC_WIDTH,
    RG_WIDTH,
    RG_WIDTH,
    N_BRANCH * D_MODEL,
)
N_IN = sum(IN_WIDTHS)

kernel_name = 'hybrid_gla_shortconv_rglru_step'


def rms_norm(x, g):
    xf = x.astype(jnp.float32)
    y = xf * lax.rsqrt(jnp.mean(xf * xf, axis=-1, keepdims=True) + RMS_EPS)
    return (y * g.astype(jnp.float32)).astype(x.dtype)


def split_cols(z):
    offs = np.cumsum(np.array(IN_WIDTHS))[:-1].tolist()
    return jnp.split(z, offs, axis=-1)


def causal_dwconv(u, buf, w):
    W = w.shape[0]
    T = u.shape[1]
    full = jnp.concatenate([buf.astype(u.dtype), u], axis=1)
    y = full[:, 0:T] * w[0]
    for j in range(1, W):
        y = y + full[:, j:j + T] * w[j]
    return y, full[:, T:]


def gla_chunked(q, k, v, log_a, s0):
    B, T, H, _ = q.shape
    C = min(GLA_CHUNK, T)
    n = -(-T // C)
    pad = n * C - T

    def blocks(a):
        a = jnp.pad(a, ((0, 0), (0, pad), (0, 0), (0, 0)))
        return a.reshape(B, n, C, H, a.shape[-1]).transpose(1, 0, 3, 2, 4)

    qb, kb, vb = blocks(q), blocks(k), blocks(v)
    bb = jnp.cumsum(blocks(log_a), axis=3)
    causal = jnp.tril(jnp.ones((C, C), dtype=bool))

    def step(S, inp):
        qc, kc, vc, bc = inp
        qe = qc * jnp.exp(bc)
        ke = kc * jnp.exp(-bc)
        att = jnp.where(causal, jnp.einsum('bhtd,bhsd->bhts', qe, ke), 0.0)
        o = jnp.einsum('bhts,bhsv->bhtv', att, vc) + jnp.einsum('bhtd,bhdv->bhtv', qe, S)
        b_end = bc[:, :, -1:, :]
        kd = kc * jnp.exp(b_end - bc)
        S = jnp.exp(b_end[:, :, 0, :])[..., None] * S + jnp.einsum('bhsd,bhsv->bhdv', kd, vc)
        return S, o

    S, ob = lax.scan(step, s0, (qb, kb, vb, bb))
    o = ob.transpose(1, 0, 3, 2, 4).reshape(B, n * C, H, -1)[:, :T]
    return o, S


def rg_lru(xc, h0, w_a, b_a, w_x, b_x, lam):
    B, T, D = xc.shape
    xb = xc.reshape(B, T, RG_BLOCKS, RG_BLOCK)
    r = jax.nn.sigmoid(jnp.einsum('btni,nij->btnj', xb, w_a).reshape(B, T, D) + b_a)
    i = jax.nn.sigmoid(jnp.einsum('btni,nij->btnj', xb, w_x).reshape(B, T, D) + b_x)
    log_a = -RG_C * r.astype(jnp.float32) * jax.nn.softplus(-lam.astype(jnp.float32))
    a = jnp.exp(log_a)
    bterm = jnp.sqrt(-jnp.expm1(2.0 * log_a)) * (i * xc).astype(jnp.float32)
    bterm = bterm.at[:, 0].add(a[:, 0] * h0.astype(jnp.float32))

    def comb(left, right):
        a1, b1 = left
        a2, b2 = right
        return a1 * a2, a2 * b1 + b2

    _, h = lax.associative_scan(comb, (a, bterm), axis=1)
    return h, h[:, -1]


def mixer_block(x, c, s_gla, s_sc, s_rgc, s_rgh,
                w_ada, b_ada, norm_g, w_in, gla_w_a, gla_b_a, gla_ng, gla_wb,
                sc_cw, sc_wb, rg_cw, rg_cb, rg_wa, rg_ba, rg_wx, rg_bx, rg_lam, rg_wb,
                b_merge, w_out):
    B, T, _ = x.shape
    mod = jax.nn.silu(c) @ w_ada + b_ada
    shift, scale, gate = jnp.split(mod[:, None, :], 3, axis=-1)
    h = rms_norm(x, norm_g) * (1 + scale) + shift
    (zq, zk, zv, zg_gla, zlr, zb, zc, zh, zg_sc, zx, zg_rg, zm) = split_cols(h @ w_in)

    q = zq.reshape(B, T, GLA_HEADS, GLA_DK).astype(jnp.float32) * (GLA_DK ** -0.5)
    k = zk.reshape(B, T, GLA_HEADS, GLA_DK).astype(jnp.float32)
    v = zv.reshape(B, T, GLA_HEADS, GLA_DV).astype(jnp.float32)
    log_a = jax.nn.log_sigmoid((zlr @ gla_w_a + gla_b_a).astype(jnp.float32)) / GLA_GATE_NORM
    log_a = log_a.reshape(B, T, GLA_HEADS, GLA_DK)
    o, new_gla = gla_chunked(q, k, v, log_a, s_gla.astype(jnp.float32))
    o = rms_norm(o, gla_ng).reshape(B, T, GLA_HEADS * GLA_DV).astype(x.dtype)
    y_gla = (o * jax.nn.silu(zg_gla)) @ gla_wb

    uc, new_sc = causal_dwconv(zc * zh, s_sc, sc_cw)
    y_sc = (zb * uc * jax.nn.silu(zg_sc)) @ sc_wb

    xc, new_rgc = causal_dwconv(zx, s_rgc, rg_cw)
    xc = xc + rg_cb
    hr, new_rgh = rg_lru(xc, s_rgh, rg_wa, rg_ba, rg_wx, rg_bx, rg_lam)
    y_rg = (hr.astype(x.dtype) * jax.nn.silu(zg_rg)) @ rg_wb

    g1, g2, g3 = jnp.split(jax.nn.sigmoid(zm + b_merge), 3, axis=-1)
    out = (g1 * y_gla + g2 * y_sc + g3 * y_rg) @ w_out
    return (x + gate * out, new_gla.astype(s_gla.dtype), new_sc.astype(s_sc.dtype),
            new_rgc.astype(s_rgc.dtype), new_rgh.astype(s_rgh.dtype))


def trunk(x, c, s_gla, s_sc, s_rgc, s_rgh, final_gain, layer_params):
    n_gla, n_sc, n_rgc, n_rgh = [], [], [], []
    for l in range(DEPTH):
        x, g, sc, rc, rh = mixer_block(x, c, s_gla[l], s_sc[l], s_rgc[l], s_rgh[l],
                                       *[p[l] for p in layer_params])
        n_gla.append(g)
        n_sc.append(sc)
        n_rgc.append(rc)
        n_rgh.append(rh)
    return (rms_norm(x, final_gain), jnp.stack(n_gla), jnp.stack(n_sc),
            jnp.stack(n_rgc), jnp.stack(n_rgh))


def setup_inputs(seed: int = 0) -> dict:
    key = jax.random.key(seed)
    ks = jax.random.split(key, 32)
    f32 = jnp.float32

    def nrm(k, shape, s):
        return jax.random.normal(k, shape, f32) * s

    D = D_MODEL
    HDK = GLA_HEADS * GLA_DK
    HDV = GLA_HEADS * GLA_DV
    u = jax.random.uniform(ks[27], (DEPTH, RG_WIDTH), f32, 0.9, 0.999)
    s = u ** (1.0 / RG_C)
    rg_lambda = jnp.log(s) - jnp.log1p(-s)
    return {
        'x_prompt': nrm(ks[0], (BATCH, SEQ, D), 1.0),
        'x_sample': nrm(ks[1], (DEC_BATCH, DEC_SEQ, D), 1.0),
        'c_prompt': nrm(ks[2], (BATCH, D), 1.0),
        'c_sample': nrm(ks[3], (DEC_BATCH, D), 1.0),
        'state_gla': nrm(ks[4], (DEPTH, DEC_BATCH, GLA_HEADS, GLA_DK, GLA_DV), 0.5),
        'state_sc_conv': nrm(ks[5], (DEPTH, DEC_BATCH, SC_CONV_W - 1, SC_WIDTH), 1.0),
        'state_rg_conv': nrm(ks[6], (DEPTH, DEC_BATCH, RG_CONV_W - 1, RG_WIDTH), 1.0),
        'state_rg_h': nrm(ks[7], (DEPTH, DEC_BATCH, RG_WIDTH), 1.0),
        'w_ada': nrm(ks[8], (DEPTH, D, 3 * D), 0.5 * D ** -0.5),
        'b_ada': nrm(ks[9], (DEPTH, 3 * D), 0.02),
        'norm_gain': 1.0 + nrm(ks[10], (DEPTH, D), 0.02),
        'w_in': nrm(ks[11], (DEPTH, D, N_IN), D ** -0.5),
        'gla_w_alpha': nrm(ks[12], (DEPTH, GLA_RANK, HDK), GLA_RANK ** -0.5),
        'gla_b_alpha': nrm(ks[13], (DEPTH, HDK), 0.5),
        'gla_norm_gain': 1.0 + nrm(ks[14], (DEPTH, GLA_DV), 0.02),
        'gla_w_branch': nrm(ks[15], (DEPTH, HDV, D), HDV ** -0.5),
        'sc_conv_w': nrm(ks[16], (DEPTH, SC_CONV_W, SC_WIDTH), SC_CONV_W ** -0.5),
        'sc_w_branch': nrm(ks[17], (DEPTH, SC_WIDTH, D), SC_WIDTH ** -0.5),
        'rg_conv_w': nrm(ks[18], (DEPTH, RG_CONV_W, RG_WIDTH), RG_CONV_W ** -0.5),
        'rg_conv_b': nrm(ks[19], (DEPTH, RG_WIDTH), 0.02),
        'rg_w_a': nrm(ks[20], (DEPTH, RG_BLOCKS, RG_BLOCK, RG_BLOCK), RG_BLOCK ** -0.5),
        'rg_b_a': nrm(ks[21], (DEPTH, RG_WIDTH), 0.02),
        'rg_w_x': nrm(ks[22], (DEPTH, RG_BLOCKS, RG_BLOCK, RG_BLOCK), RG_BLOCK ** -0.5),
        'rg_b_x': nrm(ks[23], (DEPTH, RG_WIDTH), 0.02),
        'rg_lambda': rg_lambda,
        'rg_w_branch': nrm(ks[24], (DEPTH, RG_WIDTH, D), RG_WIDTH ** -0.5),
        'b_merge': nrm(ks[25], (DEPTH, N_BRANCH * D), 0.02),
        'w_out': nrm(ks[26], (DEPTH, D, D), D ** -0.5),
        'final_gain': 1.0 + nrm(ks[28], (D,), 0.02),
    }


def reference(x_prompt, x_sample, c_prompt, c_sample, state_gla, state_sc_conv, state_rg_conv,
              state_rg_h, w_ada, b_ada, norm_gain, w_in, gla_w_alpha, gla_b_alpha, gla_norm_gain,
              gla_w_branch, sc_conv_w, sc_w_branch, rg_conv_w, rg_conv_b, rg_w_a, rg_b_a, rg_w_x,
              rg_b_x, rg_lambda, rg_w_branch, b_merge, w_out, final_gain):
    layer_params = (w_ada, b_ada, norm_gain, w_in, gla_w_alpha, gla_b_alpha, gla_norm_gain,
                    gla_w_branch, sc_conv_w, sc_w_branch, rg_conv_w, rg_conv_b, rg_w_a, rg_b_a,
                    rg_w_x, rg_b_x, rg_lambda, rg_w_branch, b_merge, w_out)
    bp = x_prompt.shape[0]
    dt = x_prompt.dtype
    z_gla = jnp.zeros((DEPTH, bp, GLA_HEADS, GLA_DK, GLA_DV), dt)
    z_sc = jnp.zeros((DEPTH, bp, SC_CONV_W - 1, SC_WIDTH), dt)
    z_rgc = jnp.zeros((DEPTH, bp, RG_CONV_W - 1, RG_WIDTH), dt)
    z_rgh = jnp.zeros((DEPTH, bp, RG_WIDTH), dt)
    y_prompt, gla_p, sc_p, rgc_p, rgh_p = trunk(x_prompt, c_prompt, z_gla, z_sc, z_rgc, z_rgh,
                                                final_gain, layer_params)
    y_sample, gla_s, sc_s, rgc_s, rgh_s = trunk(x_sample, c_sample, state_gla, state_sc_conv,
                                                state_rg_conv, state_rg_h, final_gain, layer_params)
    return (y_prompt, y_sample, gla_p, sc_p, rgc_p, rgh_p, gla_s, sc_s, rgc_s, rgh_s)
```

```python
import functools

import jax
import jax.numpy as jnp
from jax import lax
from jax.experimental import pallas as pl
from jax.experimental.pallas import tpu as pltpu

F32 = jnp.float32
BF16 = jnp.bfloat16

D_MODEL = 1024
DEPTH = 2
GLA_HEADS = 4
GLA_DK = 128
GLA_DV = 256
GLA_RANK = 16
GLA_GATE_NORM = 16.0
HDK = GLA_HEADS * GLA_DK
HDV = GLA_HEADS * GLA_DV
SC_CONV_W = 3
RG_CONV_W = 4
RG_BLOCKS = 8
RG_BLOCK = 128
RG_C = 8.0
RMS_EPS = 1e-6

LANES = 128
SUBLANES = 8
ROW_TILE = 256
GLA_CHUNK = 64
GLA_HALF = GLA_CHUNK // 2
STATE_BATCH_TILE = 8
VMEM_LIMIT = 60 * 1024 * 1024

OFF_GLA = 0
OFF_LR = OFF_GLA + 2 * HDK + 2 * HDV
OFF_SC = OFF_LR + GLA_RANK
OFF_RG = OFF_SC + 4 * D_MODEL
OFF_MERGE = OFF_RG + 2 * D_MODEL
N_IN = OFF_MERGE + 3 * D_MODEL


def _mm(a, w):
    return jnp.dot(a.astype(BF16), w, preferred_element_type=F32)


def _mm_nt(a, b):
    return lax.dot_general(a.astype(BF16), b.astype(BF16), (((1,), (1,)), ((), ())),
                           preferred_element_type=F32)


def _mm_tn(a, b):
    return lax.dot_general(a.astype(BF16), b.astype(BF16), (((0,), (0,)), ((), ())),
                           preferred_element_type=F32)


def _rms(x, g):
    ms = jnp.mean(x * x, axis=-1, keepdims=True)
    return x * lax.rsqrt(ms + RMS_EPS) * g


def _silu(x):
    return x * jax.nn.sigmoid(x)


def _mod_norm(x, mod, g):
    d = D_MODEL
    shift, scale = mod[:, 0:d], mod[:, d:2 * d]
    return _rms(x, g) * (1.0 + scale) + shift


def _gla_log_decay(hb, wlr_ref, gwa_ref, gba_ref):
    zlr = _mm(hb, wlr_ref[...])
    pre = _mm(zlr, gwa_ref[...]) + gba_ref[...]
    return jax.nn.log_sigmoid(pre) * (1.0 / GLA_GATE_NORM)


def _gla_out_norm_gate(o, zg, gng):
    parts = []
    for h in range(GLA_HEADS):
        sl = slice(h * GLA_DV, (h + 1) * GLA_DV)
        parts.append(_rms(o[:, sl], gng) * _silu(zg[:, sl]))
    return parts


def _rg_gates(xc, rgwax_ref, ba, bx, lam):
    xb = xc.astype(BF16)
    r_parts, i_parts = [], []
    for n in range(RG_BLOCKS):
        ri = jnp.dot(xb[:, n * RG_BLOCK:(n + 1) * RG_BLOCK], rgwax_ref[n], preferred_element_type=F32)
        r_parts.append(ri[:, 0:RG_BLOCK])
        i_parts.append(ri[:, RG_BLOCK:2 * RG_BLOCK])
    r = jax.nn.sigmoid(jnp.concatenate(r_parts, axis=1) + ba)
    ig = jax.nn.sigmoid(jnp.concatenate(i_parts, axis=1) + bx)
    softplus_neg_lam = jnp.log1p(jnp.exp(-jnp.abs(lam))) + jnp.maximum(-lam, 0.0)
    log_a = (-RG_C) * r * softplus_neg_lam
    a = jnp.exp(log_a)
    bterm = jnp.sqrt(1.0 - jnp.exp(2.0 * log_a)) * (ig * xc)
    return a, bterm


def _merge_out(hb, x, gate, y_gla, y_sc, y_rg, wm_ref, bm_ref, wout_ref):
    d = D_MODEL
    g1 = jax.nn.sigmoid(_mm(hb, wm_ref[:, 0:d]) + bm_ref[:, 0:d])
    mix = g1 * y_gla
    g2 = jax.nn.sigmoid(_mm(hb, wm_ref[:, d:2 * d]) + bm_ref[:, d:2 * d])
    mix = mix + g2 * y_sc
    g3 = jax.nn.sigmoid(_mm(hb, wm_ref[:, 2 * d:3 * d]) + bm_ref[:, 2 * d:3 * d])
    mix = mix + g3 * y_rg
    return x + gate * _mm(mix, wout_ref[...])


def _ada_body(c_ref, w_ref, b_ref, o_ref):
    o_ref[...] = _mm(_silu(c_ref[...]), w_ref[...].astype(BF16)) + b_ref[...]


def _ada_mod(c_all, w_ada, b_ada):
    rows = c_all.shape[0]
    d = D_MODEL
    return pl.pallas_call(
        _ada_body,
        grid=(DEPTH, 3),
        in_specs=[pl.BlockSpec((rows, d), lambda l, j: (0, 0)),
                  pl.BlockSpec((None, d, d), lambda l, j: (l, 0, j)),
                  pl.BlockSpec((None, 1, d), lambda l, j: (l, 0, j))],
        out_specs=pl.BlockSpec((None, rows, d), lambda l, j: (l, 0, j)),
        out_shape=jax.ShapeDtypeStruct((DEPTH, rows, 3 * d), F32),
        name="ada_mod",
    )(c_all, w_ada, b_ada.reshape(DEPTH, 1, 3 * d))


def _prompt_layer_body(x_ref, mod_ref, ng_ref, wgla_ref, wlr_ref, gwa_ref, gba_ref, gng_ref, gwb_ref,
                       wsc_ref, sccw_ref, scwb_ref, wrg_ref, rgcw_ref, rgcb_ref, rgwax_ref, rgba_ref,
                       rgbx_ref, rglam_ref, rgwb_ref, wm_ref, bm_ref, wout_ref, fg_ref,
                       y_ref, gla_ref, sct_ref, rgt_ref, rgh_ref,
                       st_ref, ubuf_ref, xbuf_ref, hcar_ref, *, final):
    d = D_MODEL
    tt = ROW_TILE
    i = pl.program_id(1)

    @pl.when(i == 0)
    def _():
        st_ref[...] = jnp.zeros(st_ref.shape, F32)
        ubuf_ref[0:SUBLANES, :] = jnp.zeros((SUBLANES, d), F32)
        xbuf_ref[0:SUBLANES, :] = jnp.zeros((SUBLANES, d), F32)
        hcar_ref[...] = jnp.zeros(hcar_ref.shape, F32)

    x = x_ref[...]
    mod = mod_ref[...]
    gate = mod[:, 2 * d:3 * d]
    hb = _mod_norm(x, mod, ng_ref[...]).astype(BF16)

    la = _gla_log_decay(hb, wlr_ref, gwa_ref, gba_ref)
    row = lax.broadcasted_iota(jnp.int32, (tt, tt), 0)
    col = lax.broadcasted_iota(jnp.int32, (tt, tt), 1)
    chunk_bits = GLA_CHUNK.bit_length() - 1
    causal = jnp.logical_and(jnp.right_shift(row, chunk_bits) == jnp.right_shift(col, chunk_bits), col <= row)
    tri = jnp.where(causal, 1.0, 0.0).astype(BF16)
    la_hi = la.astype(BF16)
    la_lo = (la - la_hi.astype(F32)).astype(BF16)
    bcum = (jnp.dot(tri, la_hi, preferred_element_type=F32)
            + jnp.dot(tri, la_lo, preferred_element_type=F32))

    zq = _mm(hb, wgla_ref[:, 0:HDK]) * (GLA_DK ** -0.5)
    zk = _mm(hb, wgla_ref[:, HDK:2 * HDK])
    zv = _mm(hb, wgla_ref[:, 2 * HDK:2 * HDK + HDV])
    n_chunk = tt // GLA_CHUNK
    o_heads = []
    for h in range(GLA_HEADS):
        ks = slice(h * GLA_DK, (h + 1) * GLA_DK)
        b = bcum[:, ks]
        b3 = b.reshape(n_chunk, GLA_CHUNK, GLA_DK)
        b_mid = jnp.broadcast_to(b3[:, GLA_HALF - 1:GLA_HALF, :], b3.shape).reshape(tt, GLA_DK)
        b_end = jnp.broadcast_to(b3[:, GLA_CHUNK - 1:GLA_CHUNK, :], b3.shape).reshape(tt, GLA_DK)
        q = zq[:, ks]
        k = zk[:, ks]
        qe = q * jnp.exp(b - b_mid)
        ke = k * jnp.exp(b_mid - b)
        qs = (q * jnp.exp(b)).astype(BF16)
        kd = (k * jnp.exp(b_end - b)).astype(BF16)
        att = jnp.where(causal, _mm_nt(qe, ke), 0.0)
        vb = zv[:, h * GLA_DV:(h + 1) * GLA_DV].astype(BF16)
        o = jnp.dot(att.astype(BF16), vb, preferred_element_type=F32)
        st = st_ref[h]
        o_parts = []
        for c in range(n_chunk):
            rs = slice(c * GLA_CHUNK, (c + 1) * GLA_CHUNK)
            o_parts.append(o[rs] + _mm_nt(qs[rs], st))
            e_end = jnp.exp(b3[c, GLA_CHUNK - 1:GLA_CHUNK, :])
            st = st * e_end + _mm_tn(vb[rs], kd[rs])
        st_ref[h] = st
        o_heads.append(jnp.concatenate(o_parts, axis=0))

    @pl.when(i == pl.num_programs(1) - 1)
    def _():
        for h in range(GLA_HEADS):
            gla_ref[h] = st_ref[h].T

    zg = _mm(hb, wgla_ref[:, 2 * HDK + HDV:2 * HDK + 2 * HDV])
    y_gla = None
    gng = gng_ref[...]
    for h in range(GLA_HEADS):
        sl = slice(h * GLA_DV, (h + 1) * GLA_DV)
        part = _mm(_rms(o_heads[h], gng) * _silu(zg[:, sl]), gwb_ref[sl, :])
        y_gla = part if y_gla is None else y_gla + part

    u = _mm(hb, wsc_ref[:, d:2 * d]) * _mm(hb, wsc_ref[:, 2 * d:3 * d])
    ubuf_ref[SUBLANES:SUBLANES + tt, :] = u
    cw = sccw_ref[...]
    uc = (ubuf_ref[SUBLANES - 2:SUBLANES - 2 + tt, :] * cw[0:1]
          + ubuf_ref[SUBLANES - 1:SUBLANES - 1 + tt, :] * cw[1:2]
          + u * cw[2:3])
    ubuf_ref[0:SUBLANES, :] = ubuf_ref[tt:tt + SUBLANES, :]
    sct_ref[...] = ubuf_ref[SUBLANES - (SC_CONV_W - 1):SUBLANES, :]
    zb = _mm(hb, wsc_ref[:, 0:d])
    zgs = _mm(hb, wsc_ref[:, 3 * d:4 * d])
    y_sc = _mm(zb * uc * _silu(zgs), scwb_ref[...])

    zx = _mm(hb, wrg_ref[:, 0:d])
    xbuf_ref[SUBLANES:SUBLANES + tt, :] = zx
    rw = rgcw_ref[...]
    xc = (xbuf_ref[SUBLANES - 3:SUBLANES - 3 + tt, :] * rw[0:1]
          + xbuf_ref[SUBLANES - 2:SUBLANES - 2 + tt, :] * rw[1:2]
          + xbuf_ref[SUBLANES - 1:SUBLANES - 1 + tt, :] * rw[2:3]
          + zx * rw[3:4] + rgcb_ref[...])
    xbuf_ref[0:SUBLANES, :] = xbuf_ref[tt:tt + SUBLANES, :]
    rgt_ref[...] = xbuf_ref[SUBLANES - (RG_CONV_W - 1):SUBLANES, :]
    a, bt = _rg_gates(xc, rgwax_ref, rgba_ref[...], rgbx_ref[...], rglam_ref[...])
    trow = lax.broadcasted_iota(jnp.int32, (tt, d), 0)
    step = 1
    while step < tt:
        a_prev = jnp.where(trow >= step, pltpu.roll(a, step, 0), 1.0)
        b_prev = jnp.where(trow >= step, pltpu.roll(bt, step, 0), 0.0)
        bt = a * b_prev + bt
        a = a * a_prev
        step *= 2
    hr = a * hcar_ref[...] + bt
    hcar_ref[...] = hr[tt - 1:tt, :]
    rgh_ref[...] = hr[tt - 1:tt, :]
    zgr = _mm(hb, wrg_ref[:, d:2 * d])
    y_rg = _mm(hr * _silu(zgr), rgwb_ref[...])

    xn = _merge_out(hb, x, gate, y_gla, y_sc, y_rg, wm_ref, bm_ref, wout_ref)
    if final:
        xn = _rms(xn, fg_ref[...])
    y_ref[...] = xn


def _const_spec(shape):
    nd = len(shape)
    return pl.BlockSpec(shape, lambda b, i: (0,) * nd, pipeline_mode=pl.Buffered(1))


def _prompt_layer(x, mod, p, final_gain, final):
    bsz, seq, d = x.shape
    tt = ROW_TILE
    weights = (p["ng"], p["wgla"], p["wlr"], p["gwa"], p["gba"], p["gng"], p["gwb"],
               p["wsc"], p["sccw"], p["scwb"], p["wrg"], p["rgcw"], p["rgcb"], p["rgwax"], p["rgba"],
               p["rgbx"], p["rglam"], p["rgwb"], p["wm"], p["bm"], p["wout"], final_gain)
    in_specs = [pl.BlockSpec((None, tt, d), lambda b, i: (b, i, 0)),
                pl.BlockSpec((None, 1, 3 * d), lambda b, i: (b, 0, 0))]
    in_specs += [_const_spec(w.shape) for w in weights]
    out_specs = [pl.BlockSpec((None, tt, d), lambda b, i: (b, i, 0)),
                 pl.BlockSpec((None, GLA_HEADS, GLA_DK, GLA_DV), lambda b, i: (b, 0, 0, 0)),
                 pl.BlockSpec((None, SC_CONV_W - 1, d), lambda b, i: (b, 0, 0)),
                 pl.BlockSpec((None, RG_CONV_W - 1, d), lambda b, i: (b, 0, 0)),
                 pl.BlockSpec((None, 1, d), lambda b, i: (b, 0, 0))]
    out_shape = [jax.ShapeDtypeStruct((bsz, seq, d), F32),
                 jax.ShapeDtypeStruct((bsz, GLA_HEADS, GLA_DK, GLA_DV), F32),
                 jax.ShapeDtypeStruct((bsz, SC_CONV_W - 1, d), F32),
                 jax.ShapeDtypeStruct((bsz, RG_CONV_W - 1, d), F32),
                 jax.ShapeDtypeStruct((bsz, 1, d), F32)]
    scratch = [pltpu.VMEM((GLA_HEADS, GLA_DV, GLA_DK), F32),
               pltpu.VMEM((SUBLANES + tt, d), F32),
               pltpu.VMEM((SUBLANES + tt, d), F32),
               pltpu.VMEM((1, d), F32)]
    return pl.pallas_call(
        functools.partial(_prompt_layer_body, final=final),
        grid=(bsz, seq // tt),
        in_specs=in_specs,
        out_specs=out_specs,
        out_shape=out_shape,
        scratch_shapes=scratch,
        compiler_params=pltpu.CompilerParams(
            dimension_semantics=("arbitrary", "arbitrary"), vmem_limit_bytes=VMEM_LIMIT),
        name="prompt_layer",
    )(x, mod, *weights)


def _sample_gla_prep_body(x_ref, mod_ref, ng_ref, wgla_ref, wlr_ref, gwa_ref, gba_ref,
                          q_ref, k_ref, v_ref, ea_ref):
    hb = _mod_norm(x_ref[...], mod_ref[...], ng_ref[...]).astype(BF16)
    la = _gla_log_decay(hb, wlr_ref, gwa_ref, gba_ref)
    ea_ref[...] = jnp.exp(la)
    q_ref[...] = _mm(hb, wgla_ref[:, 0:HDK]) * (GLA_DK ** -0.5)
    k_ref[...] = _mm(hb, wgla_ref[:, HDK:2 * HDK])
    v_ref[...] = _mm(hb, wgla_ref[:, 2 * HDK:2 * HDK + HDV])


def _sample_gla_prep(x, mod, p):
    rows = x.shape[0]
    return pl.pallas_call(
        _sample_gla_prep_body,
        out_shape=[jax.ShapeDtypeStruct((rows, HDK), F32),
                   jax.ShapeDtypeStruct((rows, HDK), F32),
                   jax.ShapeDtypeStruct((rows, HDV), F32),
                   jax.ShapeDtypeStruct((rows, HDK), F32)],
        compiler_params=pltpu.CompilerParams(vmem_limit_bytes=VMEM_LIMIT),
        name="sample_gla_prep",
    )(x, mod, p["ng"], p["wgla"], p["wlr"], p["gwa"], p["gba"])


def _sample_gla_state_body(q_ref, k_ref, ea_ref, v_ref, s_ref, o_ref, so_ref):
    for h in range(GLA_HEADS):
        ks = slice(h * GLA_DK, (h + 1) * GLA_DK)
        vs = slice(h * GLA_DV, (h + 1) * GLA_DV)
        q_t = q_ref[:, ks].T
        k_t = k_ref[:, ks].T
        e_t = ea_ref[:, ks].T
        for j in range(STATE_BATCH_TILE):
            s_new = e_t[:, j:j + 1] * s_ref[j, h] + k_t[:, j:j + 1] * v_ref[j:j + 1, vs]
            so_ref[j, h] = s_new
            o_ref[j:j + 1, vs] = jnp.sum(q_t[:, j:j + 1] * s_new, axis=0, keepdims=True)


def _sample_gla_state(q, k, ea, v, s):
    rows = q.shape[0]
    bt = STATE_BATCH_TILE
    return pl.pallas_call(
        _sample_gla_state_body,
        grid=(rows // bt,),
        in_specs=[pl.BlockSpec((bt, HDK), lambda i: (i, 0)),
                  pl.BlockSpec((bt, HDK), lambda i: (i, 0)),
                  pl.BlockSpec((bt, HDK), lambda i: (i, 0)),
                  pl.BlockSpec((bt, HDV), lambda i: (i, 0)),
                  pl.BlockSpec((bt, GLA_HEADS, GLA_DK, GLA_DV), lambda i: (i, 0, 0, 0))],
        out_specs=[pl.BlockSpec((bt, HDV), lambda i: (i, 0)),
                   pl.BlockSpec((bt, GLA_HEADS, GLA_DK, GLA_DV), lambda i: (i, 0, 0, 0))],
        out_shape=[jax.ShapeDtypeStruct((rows, HDV), F32),
                   jax.ShapeDtypeStruct(s.shape, F32)],
        name="sample_gla_state",
    )(q, k, ea, v, s)


def _sample_tail_body(x_ref, mod_ref, o_ref, ssc_ref, srg_ref, srh_ref,
                      ng_ref, wgla_ref, gng_ref, gwb_ref, wsc_ref, sccw_ref, scwb_ref, wrg_ref, rgcw_ref,
                      rgcb_ref, rgwax_ref, rgba_ref, rgbx_ref, rglam_ref, rgwb_ref, wm_ref, bm_ref, wout_ref,
                      fg_ref, y_ref, nsc_ref, nrg_ref, nrh_ref, *, final):
    d = D_MODEL
    x = x_ref[...]
    mod = mod_ref[...]
    gate = mod[:, 2 * d:3 * d]
    hb = _mod_norm(x, mod, ng_ref[...]).astype(BF16)

    zg = _mm(hb, wgla_ref[:, 2 * HDK + HDV:2 * HDK + 2 * HDV])
    parts = _gla_out_norm_gate(o_ref[...], zg, gng_ref[...])
    y_gla = _mm(jnp.concatenate(parts, axis=1), gwb_ref[...])

    u = _mm(hb, wsc_ref[:, d:2 * d]) * _mm(hb, wsc_ref[:, 2 * d:3 * d])
    cw = sccw_ref[...]
    s0, s1 = ssc_ref[:, 0:d], ssc_ref[:, d:2 * d]
    uc = s0 * cw[0:1] + s1 * cw[1:2] + u * cw[2:3]
    nsc_ref[:, 0:d] = s1
    nsc_ref[:, d:2 * d] = u
    zb = _mm(hb, wsc_ref[:, 0:d])
    zgs = _mm(hb, wsc_ref[:, 3 * d:4 * d])
    y_sc = _mm(zb * uc * _silu(zgs), scwb_ref[...])

    zx = _mm(hb, wrg_ref[:, 0:d])
    rw = rgcw_ref[...]
    r0, r1, r2 = srg_ref[:, 0:d], srg_ref[:, d:2 * d], srg_ref[:, 2 * d:3 * d]
    xc = r0 * rw[0:1] + r1 * rw[1:2] + r2 * rw[2:3] + zx * rw[3:4] + rgcb_ref[...]
    nrg_ref[:, 0:d] = r1
    nrg_ref[:, d:2 * d] = r2
    nrg_ref[:, 2 * d:3 * d] = zx
    a, bt = _rg_gates(xc, rgwax_ref, rgba_ref[...], rgbx_ref[...], rglam_ref[...])
    hr = a * srh_ref[...] + bt
    nrh_ref[...] = hr
    zgr = _mm(hb, wrg_ref[:, d:2 * d])
    y_rg = _mm(hr * _silu(zgr), rgwb_ref[...])

    xn = _merge_out(hb, x, gate, y_gla, y_sc, y_rg, wm_ref, bm_ref, wout_ref)
    if final:
        xn = _rms(xn, fg_ref[...])
    y_ref[...] = xn


def _sample_tail(x, mod, o, ssc, srg, srh, p, final_gain, final):
    rows, d = x.shape
    return pl.pallas_call(
        functools.partial(_sample_tail_body, final=final),
        out_shape=[jax.ShapeDtypeStruct((rows, d), F32),
                   jax.ShapeDtypeStruct((rows, (SC_CONV_W - 1) * d), F32),
                   jax.ShapeDtypeStruct((rows, (RG_CONV_W - 1) * d), F32),
                   jax.ShapeDtypeStruct((rows, d), F32)],
        compiler_params=pltpu.CompilerParams(vmem_limit_bytes=VMEM_LIMIT),
        name="sample_tail",
    )(x, mod, o, ssc, srg, srh,
      p["ng"], p["wgla"], p["gng"], p["gwb"], p["wsc"], p["sccw"], p["scwb"], p["wrg"], p["rgcw"],
      p["rgcb"], p["rgwax"], p["rgba"], p["rgbx"], p["rglam"], p["rgwb"], p["wm"], p["bm"], p["wout"],
      final_gain)


def _layer_params(l, norm_gain, w_in, gla_w_alpha, gla_b_alpha, gla_norm_gain, gla_w_branch, sc_conv_w,
                  sc_w_branch, rg_conv_w, rg_conv_b, rg_w_a, rg_b_a, rg_w_x, rg_b_x, rg_lambda,
                  rg_w_branch, b_merge, w_out):
    d = D_MODEL
    w = w_in[l]
    wlr = jnp.zeros((d, LANES), BF16).at[:, 0:GLA_RANK].set(w[:, OFF_LR:OFF_SC].astype(BF16))
    gwa = jnp.zeros((LANES, HDK), BF16).at[0:GLA_RANK, :].set(gla_w_alpha[l].astype(BF16))
    return {
        "ng": norm_gain[l].reshape(1, d),
        "wgla": w[:, OFF_GLA:OFF_LR].astype(BF16),
        "wlr": wlr,
        "gwa": gwa,
        "gba": gla_b_alpha[l].reshape(1, HDK),
        "gng": gla_norm_gain[l].reshape(1, GLA_DV),
        "gwb": gla_w_branch[l].astype(BF16),
        "wsc": w[:, OFF_SC:OFF_RG].astype(BF16),
        "sccw": sc_conv_w[l],
        "scwb": sc_w_branch[l].astype(BF16),
        "wrg": w[:, OFF_RG:OFF_MERGE].astype(BF16),
        "rgcw": rg_conv_w[l],
        "rgcb": rg_conv_b[l].reshape(1, d),
        "rgwax": jnp.concatenate([rg_w_a[l], rg_w_x[l]], axis=-1).astype(BF16),
        "rgba": rg_b_a[l].reshape(1, d),
        "rgbx": rg_b_x[l].reshape(1, d),
        "rglam": rg_lambda[l].reshape(1, d),
        "rgwb": rg_w_branch[l].astype(BF16),
        "wm": w[:, OFF_MERGE:N_IN].astype(BF16),
        "bm": b_merge[l].reshape(1, 3 * d),
        "wout": w_out[l].astype(BF16),
    }


def kernel(x_prompt, x_sample, c_prompt, c_sample, state_gla, state_sc_conv, state_rg_conv, state_rg_h, w_ada, b_ada, norm_gain, w_in, gla_w_alpha, gla_b_alpha, gla_norm_gain, gla_w_branch, sc_conv_w, sc_w_branch, rg_conv_w, rg_conv_b, rg_w_a, rg_b_a, rg_w_x, rg_b_x, rg_lambda, rg_w_branch, b_merge, w_out, final_gain):
    d = D_MODEL
    bp = x_prompt.shape[0]
    bs = x_sample.shape[0]
    fg = final_gain.reshape(1, d)
    mod_all = _ada_mod(jnp.concatenate([c_prompt, c_sample], axis=0), w_ada, b_ada)

    xp = x_prompt
    xs = x_sample.reshape(bs, d)
    gla_p, sc_p, rgc_p, rgh_p = [], [], [], []
    gla_s, sc_s, rgc_s, rgh_s = [], [], [], []
    for l in range(DEPTH):
        p = _layer_params(l, norm_gain, w_in, gla_w_alpha, gla_b_alpha, gla_norm_gain, gla_w_branch,
                          sc_conv_w, sc_w_branch, rg_conv_w, rg_conv_b, rg_w_a, rg_b_a, rg_w_x, rg_b_x,
                          rg_lambda, rg_w_branch, b_merge, w_out)
        final = l == DEPTH - 1
        mod_p = mod_all[l, 0:bp].reshape(bp, 1, 3 * d)
        mod_s = mod_all[l, bp:bp + bs]

        xp, g, sc, rc, rh = _prompt_layer(xp, mod_p, p, fg, final)
        gla_p.append(g)
        sc_p.append(sc)
        rgc_p.append(rc)
        rgh_p.append(rh.reshape(bp, d))

        q, k, v, ea = _sample_gla_prep(xs, mod_s, p)
        o, s_new = _sample_gla_state(q, k, ea, v, state_gla[l])
        xs, nsc, nrg, nrh = _sample_tail(xs, mod_s, o,
                                         state_sc_conv[l].reshape(bs, (SC_CONV_W - 1) * d),
                                         state_rg_conv[l].reshape(bs, (RG_CONV_W - 1) * d),
                                         state_rg_h[l], p, fg, final)
        gla_s.append(s_new)
        sc_s.append(nsc.reshape(bs, SC_CONV_W - 1, d))
        rgc_s.append(nrg.reshape(bs, RG_CONV_W - 1, d))
        rgh_s.append(nrh)

    return (xp, xs.reshape(bs, 1, d),
            jnp.stack(gla_p), jnp.stack(sc_p), jnp.stack(rgc_p), jnp.stack(rgh_p),
            jnp.stack(gla_s), jnp.stack(sc_s), jnp.stack(rgc_s), jnp.stack(rgh_s))
```

```python
import functools

import jax
import jax.numpy as jnp
from jax import lax
from jax.experimental import pallas as pl
from jax.experimental.pallas import tpu as pltpu

F32 = jnp.float32
BF16 = jnp.bfloat16

D_MODEL = 1024
DEPTH = 2
GLA_HEADS = 4
GLA_DK = 128
GLA_DV = 256
GLA_RANK = 16
GLA_GATE_NORM = 16.0
HDK = GLA_HEADS * GLA_DK
HDV = GLA_HEADS * GLA_DV
SC_CONV_W = 3
RG_CONV_W = 4
RG_BLOCKS = 8
RG_BLOCK = 128
RG_C = 8.0
RMS_EPS = 1e-6

LANES = 128
SUBLANES = 8
ROW_TILE = 256
GLA_CHUNK = 64
GLA_HALF = GLA_CHUNK // 2
COL_BLOCK = 256
STATE_BATCH_TILE = 8
VMEM_LIMIT = 60 * 1024 * 1024

OFF_GLA = 0
OFF_LR = OFF_GLA + 2 * HDK + 2 * HDV
OFF_SC = OFF_LR + GLA_RANK
OFF_RG = OFF_SC + 4 * D_MODEL
OFF_MERGE = OFF_RG + 2 * D_MODEL
N_IN = OFF_MERGE + 3 * D_MODEL


def _mm(a, w):
    return jnp.dot(a.astype(BF16), w, preferred_element_type=F32)


def _mm_nt(a, b):
    return lax.dot_general(a.astype(BF16), b.astype(BF16), (((1,), (1,)), ((), ())),
                           preferred_element_type=F32)


def _mm_tn(a, b):
    return lax.dot_general(a.astype(BF16), b.astype(BF16), (((0,), (0,)), ((), ())),
                           preferred_element_type=F32)


def _rms(x, g):
    ms = jnp.mean(x * x, axis=-1, keepdims=True)
    return x * lax.rsqrt(ms + RMS_EPS) * g


def _silu(x):
    return x * jax.nn.sigmoid(x)


def _mod_norm(x, mod, g):
    d = D_MODEL
    shift, scale = mod[:, 0:d], mod[:, d:2 * d]
    return _rms(x, g) * (1.0 + scale) + shift


def _gla_log_decay(hb, wlr_ref, gwa_ref, gba_ref):
    zlr = _mm(hb, wlr_ref[...])
    pre = _mm(zlr, gwa_ref[...]) + gba_ref[...]
    return jax.nn.log_sigmoid(pre) * (1.0 / GLA_GATE_NORM)


def _gla_out_norm_gate(o, zg, gng):
    parts = []
    for h in range(GLA_HEADS):
        sl = slice(h * GLA_DV, (h + 1) * GLA_DV)
        parts.append(_rms(o[:, sl], gng) * _silu(zg[:, sl]))
    return parts


def _col(cb, base=0):
    return slice(base + cb * COL_BLOCK, base + (cb + 1) * COL_BLOCK)


def _acc(total, part):
    return part if total is None else total + part


def _rg_gates(xc, cb, rgwax_ref, ba_ref, bx_ref, lam_ref):
    xb = xc.astype(BF16)
    per_cb = COL_BLOCK // RG_BLOCK
    r_parts, i_parts = [], []
    for n in range(per_cb):
        ri = jnp.dot(xb[:, n * RG_BLOCK:(n + 1) * RG_BLOCK], rgwax_ref[cb * per_cb + n],
                     preferred_element_type=F32)
        r_parts.append(ri[:, 0:RG_BLOCK])
        i_parts.append(ri[:, RG_BLOCK:2 * RG_BLOCK])
    cs = _col(cb)
    r = jax.nn.sigmoid(jnp.concatenate(r_parts, axis=1) + ba_ref[:, cs])
    ig = jax.nn.sigmoid(jnp.concatenate(i_parts, axis=1) + bx_ref[:, cs])
    lam = lam_ref[:, cs]
    softplus_neg_lam = jnp.log1p(jnp.exp(-jnp.abs(lam))) + jnp.maximum(-lam, 0.0)
    log_a = (-RG_C) * r * softplus_neg_lam
    a = jnp.exp(log_a)
    bterm = jnp.sqrt(1.0 - jnp.exp(2.0 * log_a)) * (ig * xc)
    return a, bterm


def _merge_out(x, gate, ys, logits, bm_ref, wout_ref):
    d = D_MODEL
    mix = None
    for n, (y, m) in enumerate(zip(ys, logits)):
        mix = _acc(mix, jax.nn.sigmoid(m + bm_ref[:, n * d:(n + 1) * d]) * y)
    return x + gate * _mm(mix, wout_ref[...])


def _ada_body(c_ref, w_ref, b_ref, o_ref):
    o_ref[...] = _mm(_silu(c_ref[...]), w_ref[...].astype(BF16)) + b_ref[...]


def _ada_mod(c_all, w_ada, b_ada):
    rows = c_all.shape[0]
    d = D_MODEL
    return pl.pallas_call(
        _ada_body,
        grid=(DEPTH, 3),
        in_specs=[pl.BlockSpec((rows, d), lambda l, j: (0, 0)),
                  pl.BlockSpec((None, d, d), lambda l, j: (l, 0, j)),
                  pl.BlockSpec((None, 1, d), lambda l, j: (l, 0, j))],
        out_specs=pl.BlockSpec((None, rows, d), lambda l, j: (l, 0, j)),
        out_shape=jax.ShapeDtypeStruct((DEPTH, rows, 3 * d), F32),
        name="ada_mod",
    )(c_all, w_ada, b_ada.reshape(DEPTH, 1, 3 * d))


def _prompt_layer_body(x_ref, mod_ref, ng_ref, wgla_ref, wlr_ref, gwa_ref, gba_ref, gng_ref, gwb_ref,
                       wsc_ref, sccw_ref, scwb_ref, wrg_ref, rgcw_ref, rgcb_ref, rgwax_ref, rgba_ref,
                       rgbx_ref, rglam_ref, rgwb_ref, wm_ref, bm_ref, wout_ref, fg_ref,
                       y_ref, gla_ref, sct_ref, rgt_ref, rgh_ref,
                       st_ref, ubuf_ref, xbuf_ref, hcar_ref, *, final):
    d = D_MODEL
    tt = ROW_TILE
    i = pl.program_id(1)

    @pl.when(i == 0)
    def _():
        st_ref[...] = jnp.zeros(st_ref.shape, F32)
        ubuf_ref[0:SUBLANES, :] = jnp.zeros((SUBLANES, d), F32)
        xbuf_ref[0:SUBLANES, :] = jnp.zeros((SUBLANES, d), F32)
        hcar_ref[...] = jnp.zeros(hcar_ref.shape, F32)

    x = x_ref[...]
    mod = mod_ref[...]
    gate = mod[:, 2 * d:3 * d]
    hb = _mod_norm(x, mod, ng_ref[...]).astype(BF16)

    t_gla = _prompt_gla_branch(hb, wgla_ref, wlr_ref, gwa_ref, gba_ref, gng_ref, st_ref)

    @pl.when(i == pl.num_programs(1) - 1)
    def _():
        for h in range(GLA_HEADS):
            gla_ref[h] = st_ref[h].T

    y_gla = None
    for h in range(GLA_HEADS):
        y_gla = _acc(y_gla, _mm(t_gla[h], gwb_ref[h * GLA_DV:(h + 1) * GLA_DV, :]))
    y_sc = _mm(_prompt_sc_branch(hb, wsc_ref, sccw_ref, ubuf_ref, sct_ref), scwb_ref[...])
    t_rg = _prompt_rg_branch(hb, wrg_ref, rgcw_ref, rgcb_ref, rgwax_ref, rgba_ref, rgbx_ref, rglam_ref,
                             xbuf_ref, hcar_ref, rgt_ref, rgh_ref)
    y_rg = _mm(t_rg, rgwb_ref[...])
    logits = tuple(_mm(hb, wm_ref[:, n * d:(n + 1) * d]) for n in range(3))
    xn = _merge_out(x, gate, (y_gla, y_sc, y_rg), logits, bm_ref, wout_ref)
    if final:
        xn = _rms(xn, fg_ref[...])
    y_ref[...] = xn


def _prompt_gla_branch(hb, wgla_ref, wlr_ref, gwa_ref, gba_ref, gng_ref, st_ref):
    tt = ROW_TILE
    la = _gla_log_decay(hb, wlr_ref, gwa_ref, gba_ref)
    row = lax.broadcasted_iota(jnp.int32, (tt, tt), 0)
    col = lax.broadcasted_iota(jnp.int32, (tt, tt), 1)
    chunk_bits = GLA_CHUNK.bit_length() - 1
    causal = jnp.logical_and(jnp.right_shift(row, chunk_bits) == jnp.right_shift(col, chunk_bits), col <= row)
    tri = jnp.where(causal, 1.0, 0.0).astype(BF16)
    la_hi = la.astype(BF16)
    la_lo = (la - la_hi.astype(F32)).astype(BF16)
    bcum = (jnp.dot(tri, la_hi, preferred_element_type=F32)
            + jnp.dot(tri, la_lo, preferred_element_type=F32))

    zq = _mm(hb, wgla_ref[:, 0:HDK]) * (GLA_DK ** -0.5)
    zk = _mm(hb, wgla_ref[:, HDK:2 * HDK])
    zv = _mm(hb, wgla_ref[:, 2 * HDK:2 * HDK + HDV])
    zg = _mm(hb, wgla_ref[:, 2 * HDK + HDV:2 * HDK + 2 * HDV])
    gng = gng_ref[...]
    n_chunk = tt // GLA_CHUNK
    gated = []
    for h in range(GLA_HEADS):
        ks = slice(h * GLA_DK, (h + 1) * GLA_DK)
        vs = slice(h * GLA_DV, (h + 1) * GLA_DV)
        b = bcum[:, ks]
        b3 = b.reshape(n_chunk, GLA_CHUNK, GLA_DK)
        b_mid = jnp.broadcast_to(b3[:, GLA_HALF - 1:GLA_HALF, :], b3.shape).reshape(tt, GLA_DK)
        b_end = jnp.broadcast_to(b3[:, GLA_CHUNK - 1:GLA_CHUNK, :], b3.shape).reshape(tt, GLA_DK)
        q = zq[:, ks]
        k = zk[:, ks]
        qe = q * jnp.exp(b - b_mid)
        ke = k * jnp.exp(b_mid - b)
        qs = (q * jnp.exp(b)).astype(BF16)
        kd = (k * jnp.exp(b_end - b)).astype(BF16)
        att = jnp.where(causal, _mm_nt(qe, ke), 0.0)
        vb = zv[:, vs].astype(BF16)
        o = jnp.dot(att.astype(BF16), vb, preferred_element_type=F32)
        st = st_ref[h]
        o_parts = []
        for c in range(n_chunk):
            rs = slice(c * GLA_CHUNK, (c + 1) * GLA_CHUNK)
            o_parts.append(o[rs] + _mm_nt(qs[rs], st))
            e_end = jnp.exp(b3[c, GLA_CHUNK - 1:GLA_CHUNK, :])
            st = st * e_end + _mm_tn(vb[rs], kd[rs])
        st_ref[h] = st
        gated.append(_rms(jnp.concatenate(o_parts, axis=0), gng) * _silu(zg[:, vs]))
    return gated


def _prompt_sc_branch(hb, wsc_ref, sccw_ref, ubuf_ref, sct_ref):
    d = D_MODEL
    tt = ROW_TILE
    u = _mm(hb, wsc_ref[:, d:2 * d]) * _mm(hb, wsc_ref[:, 2 * d:3 * d])
    ubuf_ref[SUBLANES:SUBLANES + tt, :] = u
    cw = sccw_ref[...]
    uc = (ubuf_ref[SUBLANES - 2:SUBLANES - 2 + tt, :] * cw[0:1]
          + ubuf_ref[SUBLANES - 1:SUBLANES - 1 + tt, :] * cw[1:2]
          + u * cw[2:3])
    ubuf_ref[0:SUBLANES, :] = ubuf_ref[tt:tt + SUBLANES, :]
    sct_ref[...] = ubuf_ref[SUBLANES - (SC_CONV_W - 1):SUBLANES, :]
    zb = _mm(hb, wsc_ref[:, 0:d])
    zgs = _mm(hb, wsc_ref[:, 3 * d:4 * d])
    return zb * uc * _silu(zgs)


def _prompt_rg_branch(hb, wrg_ref, rgcw_ref, rgcb_ref, rgwax_ref, rgba_ref, rgbx_ref, rglam_ref,
                      xbuf_ref, hcar_ref, rgt_ref, rgh_ref):
    d = D_MODEL
    tt = ROW_TILE
    zx = _mm(hb, wrg_ref[:, 0:d])
    xbuf_ref[SUBLANES:SUBLANES + tt, :] = zx
    rw = rgcw_ref[...]
    xc = (xbuf_ref[SUBLANES - 3:SUBLANES - 3 + tt, :] * rw[0:1]
          + xbuf_ref[SUBLANES - 2:SUBLANES - 2 + tt, :] * rw[1:2]
          + xbuf_ref[SUBLANES - 1:SUBLANES - 1 + tt, :] * rw[2:3]
          + zx * rw[3:4] + rgcb_ref[...])
    xbuf_ref[0:SUBLANES, :] = xbuf_ref[tt:tt + SUBLANES, :]
    rgt_ref[...] = xbuf_ref[SUBLANES - (RG_CONV_W - 1):SUBLANES, :]
    groups = tt // SUBLANES
    sub = lax.broadcasted_iota(jnp.int32, (1, SUBLANES, COL_BLOCK), 1)
    h_cols = []
    for cb in range(d // COL_BLOCK):
        cs = _col(cb)
        a, bt = _rg_gates(xc[:, cs], cb, rgwax_ref, rgba_ref, rgbx_ref, rglam_ref)
        a3 = a.reshape(groups, SUBLANES, COL_BLOCK)
        b3 = bt.reshape(groups, SUBLANES, COL_BLOCK)
        step = 1
        while step < SUBLANES:
            a_prev = jnp.where(sub >= step, pltpu.roll(a3, step, 1), 1.0)
            b_prev = jnp.where(sub >= step, pltpu.roll(b3, step, 1), 0.0)
            b3 = a3 * b_prev + b3
            a3 = a3 * a_prev
            step *= 2
        h_row = hcar_ref[:, cs]
        h_groups = []
        for g in range(groups):
            h_g = a3[g] * h_row + b3[g]
            h_groups.append(h_g)
            h_row = h_g[SUBLANES - 1:SUBLANES, :]
        h_cols.append(jnp.concatenate(h_groups, axis=0))
        hcar_ref[:, cs] = h_row
        rgh_ref[:, cs] = h_row
    hr = jnp.concatenate(h_cols, axis=1)
    return hr * _silu(_mm(hb, wrg_ref[:, d:2 * d]))


def _const_spec(shape):
    nd = len(shape)
    return pl.BlockSpec(shape, lambda b, i: (0,) * nd, pipeline_mode=pl.Buffered(1))


def _prompt_layer(x, mod, p, final_gain, final):
    bsz, seq, d = x.shape
    tt = ROW_TILE
    weights = (p["ng"], p["wgla"], p["wlr"], p["gwa"], p["gba"], p["gng"], p["gwb"],
               p["wsc"], p["sccw"], p["scwb"], p["wrg"], p["rgcw"], p["rgcb"], p["rgwax"], p["rgba"],
               p["rgbx"], p["rglam"], p["rgwb"], p["wm"], p["bm"], p["wout"], final_gain)
    in_specs = [pl.BlockSpec((None, tt, d), lambda b, i: (b, i, 0)),
                pl.BlockSpec((None, 1, 3 * d), lambda b, i: (b, 0, 0))]
    in_specs += [_const_spec(w.shape) for w in weights]
    out_specs = [pl.BlockSpec((None, tt, d), lambda b, i: (b, i, 0)),
                 pl.BlockSpec((None, GLA_HEADS, GLA_DK, GLA_DV), lambda b, i: (b, 0, 0, 0)),
                 pl.BlockSpec((None, SC_CONV_W - 1, d), lambda b, i: (b, 0, 0)),
                 pl.BlockSpec((None, RG_CONV_W - 1, d), lambda b, i: (b, 0, 0)),
                 pl.BlockSpec((None, 1, d), lambda b, i: (b, 0, 0))]
    out_shape = [jax.ShapeDtypeStruct((bsz, seq, d), F32),
                 jax.ShapeDtypeStruct((bsz, GLA_HEADS, GLA_DK, GLA_DV), F32),
                 jax.ShapeDtypeStruct((bsz, SC_CONV_W - 1, d), F32),
                 jax.ShapeDtypeStruct((bsz, RG_CONV_W - 1, d), F32),
                 jax.ShapeDtypeStruct((bsz, 1, d), F32)]
    scratch = [pltpu.VMEM((GLA_HEADS, GLA_DV, GLA_DK), F32),
               pltpu.VMEM((SUBLANES + tt, d), F32),
               pltpu.VMEM((SUBLANES + tt, d), F32),
               pltpu.VMEM((1, d), F32)]
    return pl.pallas_call(
        functools.partial(_prompt_layer_body, final=final),
        grid=(bsz, seq // tt),
        in_specs=in_specs,
        out_specs=out_specs,
        out_shape=out_shape,
        scratch_shapes=scratch,
        compiler_params=pltpu.CompilerParams(
            dimension_semantics=("arbitrary", "arbitrary"), vmem_limit_bytes=VMEM_LIMIT),
        name="prompt_layer",
    )(x, mod, *weights)


def _sample_gla_prep_body(x_ref, mod_ref, ng_ref, wgla_ref, wlr_ref, gwa_ref, gba_ref,
                          q_ref, k_ref, v_ref, ea_ref):
    hb = _mod_norm(x_ref[...], mod_ref[...], ng_ref[...]).astype(BF16)
    la = _gla_log_decay(hb, wlr_ref, gwa_ref, gba_ref)
    ea_ref[...] = jnp.exp(la)
    q_ref[...] = _mm(hb, wgla_ref[:, 0:HDK]) * (GLA_DK ** -0.5)
    k_ref[...] = _mm(hb, wgla_ref[:, HDK:2 * HDK])
    v_ref[...] = _mm(hb, wgla_ref[:, 2 * HDK:2 * HDK + HDV])


def _sample_gla_prep(x, mod, p):
    rows = x.shape[0]
    return pl.pallas_call(
        _sample_gla_prep_body,
        out_shape=[jax.ShapeDtypeStruct((rows, HDK), F32),
                   jax.ShapeDtypeStruct((rows, HDK), F32),
                   jax.ShapeDtypeStruct((rows, HDV), F32),
                   jax.ShapeDtypeStruct((rows, HDK), F32)],
        compiler_params=pltpu.CompilerParams(vmem_limit_bytes=VMEM_LIMIT),
        name="sample_gla_prep",
    )(x, mod, p["ng"], p["wgla"], p["wlr"], p["gwa"], p["gba"])


def _sample_gla_state_body(q_ref, k_ref, ea_ref, v_ref, s_ref, *rest):
    o_ref, so_ref = rest[-2:]
    for h in range(GLA_HEADS):
        ks = slice(h * GLA_DK, (h + 1) * GLA_DK)
        vs = slice(h * GLA_DV, (h + 1) * GLA_DV)
        q_t = q_ref[:, ks].T
        k_t = k_ref[:, ks].T
        e_t = ea_ref[:, ks].T
        for j in range(STATE_BATCH_TILE):
            s_new = e_t[:, j:j + 1] * s_ref[j, h] + k_t[:, j:j + 1] * v_ref[j:j + 1, vs]
            so_ref[j, h] = s_new
            o_ref[j:j + 1, vs] = jnp.sum(q_t[:, j:j + 1] * s_new, axis=0, keepdims=True)


def _sample_gla_state(q, k, ea, v, state, layer, carrier):
    rows = q.shape[0]
    bt = STATE_BATCH_TILE
    state_block = (None, bt, GLA_HEADS, GLA_DK, GLA_DV)
    state_index = lambda i: (layer, i, 0, 0, 0)
    args = [q, k, ea, v, state]
    in_specs = [pl.BlockSpec((bt, HDK), lambda i: (i, 0)),
                pl.BlockSpec((bt, HDK), lambda i: (i, 0)),
                pl.BlockSpec((bt, HDK), lambda i: (i, 0)),
                pl.BlockSpec((bt, HDV), lambda i: (i, 0)),
                pl.BlockSpec(state_block, state_index)]
    aliases = {}
    if carrier is not None:
        args.append(carrier)
        in_specs.append(pl.BlockSpec(memory_space=pl.ANY))
        aliases = {len(args) - 1: 1}
    return pl.pallas_call(
        _sample_gla_state_body,
        grid=(rows // bt,),
        in_specs=in_specs,
        out_specs=[pl.BlockSpec((bt, HDV), lambda i: (i, 0)),
                   pl.BlockSpec(state_block, state_index)],
        out_shape=[jax.ShapeDtypeStruct((rows, HDV), F32),
                   jax.ShapeDtypeStruct(state.shape, F32)],
        input_output_aliases=aliases,
        name="sample_gla_state",
    )(*args)


def _sample_tail_body(x_ref, mod_ref, o_ref, ssc_ref, srg_ref, srh_ref,
                      ng_ref, wgla_ref, gng_ref, gwb_ref, wsc_ref, sccw_ref, scwb_ref, wrg_ref, rgcw_ref,
                      rgcb_ref, rgwax_ref, rgba_ref, rgbx_ref, rglam_ref, rgwb_ref, wm_ref, bm_ref, wout_ref,
                      fg_ref, y_ref, nsc_ref, nrg_ref, nrh_ref, *, final):
    d = D_MODEL
    x = x_ref[...]
    mod = mod_ref[...]
    gate = mod[:, 2 * d:3 * d]
    hb = _mod_norm(x, mod, ng_ref[...]).astype(BF16)

    zg = _mm(hb, wgla_ref[:, 2 * HDK + HDV:2 * HDK + 2 * HDV])
    parts = _gla_out_norm_gate(o_ref[...], zg, gng_ref[...])
    y_gla = _mm(jnp.concatenate(parts, axis=1), gwb_ref[...])

    u = _mm(hb, wsc_ref[:, d:2 * d]) * _mm(hb, wsc_ref[:, 2 * d:3 * d])
    cw = sccw_ref[...]
    s0, s1 = ssc_ref[:, 0:d], ssc_ref[:, d:2 * d]
    uc = s0 * cw[0:1] + s1 * cw[1:2] + u * cw[2:3]
    nsc_ref[:, 0:d] = s1
    nsc_ref[:, d:2 * d] = u
    zb = _mm(hb, wsc_ref[:, 0:d])
    zgs = _mm(hb, wsc_ref[:, 3 * d:4 * d])
    y_sc = _mm(zb * uc * _silu(zgs), scwb_ref[...])

    zx = _mm(hb, wrg_ref[:, 0:d])
    rw = rgcw_ref[...]
    r0, r1, r2 = srg_ref[:, 0:d], srg_ref[:, d:2 * d], srg_ref[:, 2 * d:3 * d]
    xc = r0 * rw[0:1] + r1 * rw[1:2] + r2 * rw[2:3] + zx * rw[3:4] + rgcb_ref[...]
    nrg_ref[:, 0:d] = r1
    nrg_ref[:, d:2 * d] = r2
    nrg_ref[:, 2 * d:3 * d] = zx
    h_parts = []
    for cb in range(d // COL_BLOCK):
        cs = _col(cb)
        a, bt = _rg_gates(xc[:, cs], cb, rgwax_ref, rgba_ref, rgbx_ref, rglam_ref)
        h_parts.append(a * srh_ref[:, cs] + bt)
    hr = jnp.concatenate(h_parts, axis=1)
    nrh_ref[...] = hr
    zgr = _mm(hb, wrg_ref[:, d:2 * d])
    y_rg = _mm(hr * _silu(zgr), rgwb_ref[...])

    logits = tuple(_mm(hb, wm_ref[:, n * d:(n + 1) * d]) for n in range(3))
    xn = _merge_out(x, gate, (y_gla, y_sc, y_rg), logits, bm_ref, wout_ref)
    if final:
        xn = _rms(xn, fg_ref[...])
    y_ref[...] = xn


def _sample_tail(x, mod, o, ssc, srg, srh, p, final_gain, final):
    rows, d = x.shape
    return pl.pallas_call(
        functools.partial(_sample_tail_body, final=final),
        out_shape=[jax.ShapeDtypeStruct((rows, d), F32),
                   jax.ShapeDtypeStruct((rows, (SC_CONV_W - 1) * d), F32),
                   jax.ShapeDtypeStruct((rows, (RG_CONV_W - 1) * d), F32),
                   jax.ShapeDtypeStruct((rows, d), F32)],
        compiler_params=pltpu.CompilerParams(vmem_limit_bytes=VMEM_LIMIT),
        name="sample_tail",
    )(x, mod, o, ssc, srg, srh,
      p["ng"], p["wgla"], p["gng"], p["gwb"], p["wsc"], p["sccw"], p["scwb"], p["wrg"], p["rgcw"],
      p["rgcb"], p["rgwax"], p["rgba"], p["rgbx"], p["rglam"], p["rgwb"], p["wm"], p["bm"], p["wout"],
      final_gain)


def _layer_params(l, norm_gain, w_in, gla_w_alpha, gla_b_alpha, gla_norm_gain, gla_w_branch, sc_conv_w,
                  sc_w_branch, rg_conv_w, rg_conv_b, rg_w_a, rg_b_a, rg_w_x, rg_b_x, rg_lambda,
                  rg_w_branch, b_merge, w_out):
    d = D_MODEL
    w = w_in[l]
    wlr = jnp.zeros((d, LANES), BF16).at[:, 0:GLA_RANK].set(w[:, OFF_LR:OFF_SC].astype(BF16))
    gwa = jnp.zeros((LANES, HDK), BF16).at[0:GLA_RANK, :].set(gla_w_alpha[l].astype(BF16))
    return {
        "ng": norm_gain[l].reshape(1, d),
        "wgla": w[:, OFF_GLA:OFF_LR].astype(BF16),
        "wlr": wlr,
        "gwa": gwa,
        "gba": gla_b_alpha[l].reshape(1, HDK),
        "gng": gla_norm_gain[l].reshape(1, GLA_DV),
        "gwb": gla_w_branch[l].astype(BF16),
        "wsc": w[:, OFF_SC:OFF_RG].astype(BF16),
        "sccw": sc_conv_w[l],
        "scwb": sc_w_branch[l].astype(BF16),
        "wrg": w[:, OFF_RG:OFF_MERGE].astype(BF16),
        "rgcw": rg_conv_w[l],
        "rgcb": rg_conv_b[l].reshape(1, d),
        "rgwax": jnp.concatenate([rg_w_a[l], rg_w_x[l]], axis=-1).astype(BF16),
        "rgba": rg_b_a[l].reshape(1, d),
        "rgbx": rg_b_x[l].reshape(1, d),
        "rglam": rg_lambda[l].reshape(1, d),
        "rgwb": rg_w_branch[l].astype(BF16),
        "wm": w[:, OFF_MERGE:N_IN].astype(BF16),
        "bm": b_merge[l].reshape(1, 3 * d),
        "wout": w_out[l].astype(BF16),
    }


def kernel(x_prompt, x_sample, c_prompt, c_sample, state_gla, state_sc_conv, state_rg_conv, state_rg_h, w_ada, b_ada, norm_gain, w_in, gla_w_alpha, gla_b_alpha, gla_norm_gain, gla_w_branch, sc_conv_w, sc_w_branch, rg_conv_w, rg_conv_b, rg_w_a, rg_b_a, rg_w_x, rg_b_x, rg_lambda, rg_w_branch, b_merge, w_out, final_gain):
    d = D_MODEL
    bp = x_prompt.shape[0]
    bs = x_sample.shape[0]
    fg = final_gain.reshape(1, d)
    mod_all = _ada_mod(jnp.concatenate([c_prompt, c_sample], axis=0), w_ada, b_ada)

    xp = x_prompt
    xs = x_sample.reshape(bs, d)
    gla_p, sc_p, rgc_p, rgh_p = [], [], [], []
    gla_s, sc_s, rgc_s, rgh_s = None, [], [], []
    for l in range(DEPTH):
        p = _layer_params(l, norm_gain, w_in, gla_w_alpha, gla_b_alpha, gla_norm_gain, gla_w_branch,
                          sc_conv_w, sc_w_branch, rg_conv_w, rg_conv_b, rg_w_a, rg_b_a, rg_w_x, rg_b_x,
                          rg_lambda, rg_w_branch, b_merge, w_out)
        final = l == DEPTH - 1
        mod_p = mod_all[l, 0:bp].reshape(bp, 1, 3 * d)
        mod_s = mod_all[l, bp:bp + bs]

        xp, g, sc, rc, rh = _prompt_layer(xp, mod_p, p, fg, final)
        gla_p.append(g)
        sc_p.append(sc)
        rgc_p.append(rc)
        rgh_p.append(rh.reshape(bp, d))

        q, k, v, ea = _sample_gla_prep(xs, mod_s, p)
        o, gla_s = _sample_gla_state(q, k, ea, v, state_gla, l, gla_s)
        xs, nsc, nrg, nrh = _sample_tail(xs, mod_s, o,
                                         state_sc_conv[l].reshape(bs, (SC_CONV_W - 1) * d),
                                         state_rg_conv[l].reshape(bs, (RG_CONV_W - 1) * d),
                                         state_rg_h[l], p, fg, final)
        sc_s.append(nsc.reshape(bs, SC_CONV_W - 1, d))
        rgc_s.append(nrg.reshape(bs, RG_CONV_W - 1, d))
        rgh_s.append(nrh)

    return (xp, xs.reshape(bs, 1, d),
            jnp.stack(gla_p), jnp.stack(sc_p), jnp.stack(rgc_p), jnp.stack(rgh_p),
            gla_s, jnp.stack(sc_s), jnp.stack(rgc_s), jnp.stack(rgh_s))
```

```python
import functools

import jax
import jax.numpy as jnp
from jax import lax
from jax.experimental import pallas as pl
from jax.experimental.pallas import tpu as pltpu

F32 = jnp.float32
BF16 = jnp.bfloat16

D_MODEL = 1024
DEPTH = 2
GLA_HEADS = 4
GLA_DK = 128
GLA_DV = 256
GLA_RANK = 16
GLA_GATE_NORM = 16.0
HDK = GLA_HEADS * GLA_DK
HDV = GLA_HEADS * GLA_DV
SC_CONV_W = 3
RG_CONV_W = 4
RG_BLOCKS = 8
RG_BLOCK = 128
RG_C = 8.0
RMS_EPS = 1e-6

LANES = 128
SUBLANES = 8
ROW_TILE = 256
GLA_CHUNK = 64
GLA_HALF = GLA_CHUNK // 2
COL_BLOCK = 256
SEG_STEPS = ROW_TILE // SUBLANES
SLAB_PITCH = SEG_STEPS + 4
STATE_BATCH_TILE = 8
VMEM_LIMIT = 60 * 1024 * 1024

OFF_GLA = 0
OFF_LR = OFF_GLA + 2 * HDK + 2 * HDV
OFF_SC = OFF_LR + GLA_RANK
OFF_RG = OFF_SC + 4 * D_MODEL
OFF_MERGE = OFF_RG + 2 * D_MODEL
N_IN = OFF_MERGE + 3 * D_MODEL


def _mm(a, w):
    return jnp.dot(a.astype(BF16), w, preferred_element_type=F32)


def _mm_nt(a, b):
    return lax.dot_general(a.astype(BF16), b.astype(BF16), (((1,), (1,)), ((), ())),
                           preferred_element_type=F32)


def _mm_tn(a, b):
    return lax.dot_general(a.astype(BF16), b.astype(BF16), (((0,), (0,)), ((), ())),
                           preferred_element_type=F32)


def _rms(x, g):
    ms = jnp.mean(x * x, axis=-1, keepdims=True)
    return x * lax.rsqrt(ms + RMS_EPS) * g


def _silu(x):
    return x * jax.nn.sigmoid(x)


def _mod_norm(x, mod, g):
    d = D_MODEL
    shift, scale = mod[:, 0:d], mod[:, d:2 * d]
    return _rms(x, g) * (1.0 + scale) + shift


def _gla_log_decay(hb, wlr_ref, gwa_ref, gba_ref):
    zlr = _mm(hb, wlr_ref[...])
    pre = _mm(zlr, gwa_ref[...]) + gba_ref[...]
    return jax.nn.log_sigmoid(pre) * (1.0 / GLA_GATE_NORM)


def _gla_out_norm_gate(o, zg, gng):
    parts = []
    for h in range(GLA_HEADS):
        sl = slice(h * GLA_DV, (h + 1) * GLA_DV)
        parts.append(_rms(o[:, sl], gng) * _silu(zg[:, sl]))
    return parts


def _col(cb, base=0):
    return slice(base + cb * COL_BLOCK, base + (cb + 1) * COL_BLOCK)


def _acc(total, part):
    return part if total is None else total + part


def _rg_gate_logits(xc, cb, rgwax_ref):
    xb = xc.astype(BF16)
    per_cb = COL_BLOCK // RG_BLOCK
    r_parts, i_parts = [], []
    for n in range(per_cb):
        ri = jnp.dot(xb[:, n * RG_BLOCK:(n + 1) * RG_BLOCK], rgwax_ref[cb * per_cb + n],
                     preferred_element_type=F32)
        r_parts.append(ri[:, 0:RG_BLOCK])
        i_parts.append(ri[:, RG_BLOCK:2 * RG_BLOCK])
    return jnp.concatenate(r_parts, axis=1), jnp.concatenate(i_parts, axis=1)


def _rg_coeffs(xc, r_pre, i_pre, cb, ba_ref, bx_ref, lam_ref):
    cs = _col(cb)
    r = jax.nn.sigmoid(r_pre + ba_ref[:, cs])
    ig = jax.nn.sigmoid(i_pre + bx_ref[:, cs])
    lam = lam_ref[:, cs]
    softplus_neg_lam = jnp.log1p(jnp.exp(-jnp.abs(lam))) + jnp.maximum(-lam, 0.0)
    log_a = (-RG_C) * r * softplus_neg_lam
    a = jnp.exp(log_a)
    bterm = jnp.sqrt(1.0 - jnp.exp(2.0 * log_a)) * (ig * xc)
    return a, bterm


def _merge_out(x, gate, ys, logits, bm_ref, wout_ref):
    d = D_MODEL
    mix = None
    for n, (y, m) in enumerate(zip(ys, logits)):
        mix = _acc(mix, jax.nn.sigmoid(m + bm_ref[:, n * d:(n + 1) * d]) * y)
    return x + gate * _mm(mix, wout_ref[...])


def _ada_body(c_ref, w_ref, b_ref, o_ref):
    o_ref[...] = _mm(_silu(c_ref[...]), w_ref[...].astype(BF16)) + b_ref[...]


def _ada_mod(c_all, w_ada, b_ada):
    rows = c_all.shape[0]
    d = D_MODEL
    return pl.pallas_call(
        _ada_body,
        grid=(DEPTH, 3),
        in_specs=[pl.BlockSpec((rows, d), lambda l, j: (0, 0)),
                  pl.BlockSpec((None, d, d), lambda l, j: (l, 0, j)),
                  pl.BlockSpec((None, 1, d), lambda l, j: (l, 0, j))],
        out_specs=pl.BlockSpec((None, rows, d), lambda l, j: (l, 0, j)),
        out_shape=jax.ShapeDtypeStruct((DEPTH, rows, 3 * d), F32),
        name="ada_mod",
    )(c_all, w_ada, b_ada.reshape(DEPTH, 1, 3 * d))


def _prompt_layer_body(x_ref, mod_ref, ng_ref, wgla_ref, wlr_ref, gwa_ref, gba_ref, gng_ref, gwb_ref,
                       wsc_ref, sccw_ref, scwb_ref, wrg_ref, rgcw_ref, rgcb_ref, rgwax_ref, rgba_ref,
                       rgbx_ref, rglam_ref, rgwb_ref, wm_ref, bm_ref, wout_ref, fg_ref,
                       y_ref, gla_ref, sct_ref, rgt_ref, rgh_ref,
                       st_ref, hcar_ref, slab_h_ref, slab_g_ref, slab_o_ref, *, final):
    d = D_MODEL
    i = pl.program_id(1)

    @pl.when(i == 0)
    def _():
        st_ref[...] = jnp.zeros(st_ref.shape, F32)
        sct_ref[...] = jnp.zeros(sct_ref.shape, F32)
        rgt_ref[...] = jnp.zeros(rgt_ref.shape, F32)
        hcar_ref[...] = jnp.zeros(hcar_ref.shape, F32)

    x = x_ref[...]
    mod = mod_ref[...]
    gate = mod[:, 2 * d:3 * d]
    h = _mod_norm(x, mod, ng_ref[...])
    hb = h.astype(BF16)
    hb_s = _to_segment_major(h, slab_h_ref).astype(BF16)

    t_gla = _prompt_gla_branch(hb, wgla_ref, wlr_ref, gwa_ref, gba_ref, gng_ref, st_ref)

    @pl.when(i == pl.num_programs(1) - 1)
    def _():
        for hd in range(GLA_HEADS):
            gla_ref[hd] = st_ref[hd].T

    y_gla = None
    for hd in range(GLA_HEADS):
        y_gla = _acc(y_gla, _mm(t_gla[hd], gwb_ref[hd * GLA_DV:(hd + 1) * GLA_DV, :]))
    y_sc = _mm(_prompt_sc_branch(hb_s, wsc_ref, sccw_ref, sct_ref), scwb_ref[...])
    y_rg = _prompt_rg_branch(hb_s, wrg_ref, rgcw_ref, rgcb_ref, rgwax_ref, rgba_ref, rgbx_ref, rglam_ref,
                             rgwb_ref, hcar_ref, rgt_ref, rgh_ref)
    logits = tuple(_mm(hb_s, wm_ref[:, n * d:(n + 1) * d]) for n in range(3))
    y_gla_s = _to_segment_major(y_gla, slab_g_ref)
    mix = None
    for n, (y, m) in enumerate(zip((y_gla_s, y_sc, y_rg), logits)):
        mix = _acc(mix, jax.nn.sigmoid(m + bm_ref[:, n * d:(n + 1) * d]) * y)
    xn = x + gate * _from_segment_major(_mm(mix, wout_ref[...]), slab_o_ref)
    if final:
        xn = _rms(xn, fg_ref[...])
    y_ref[...] = xn


def _to_segment_major(v, slab_ref):
    for lb in range(D_MODEL // LANES):
        for j in range(SUBLANES):
            slab_ref[lb, j * SLAB_PITCH:j * SLAB_PITCH + SEG_STEPS, :] = (
                v[j * SEG_STEPS:(j + 1) * SEG_STEPS, lb * LANES:(lb + 1) * LANES])
    steps = []
    for s in range(SEG_STEPS):
        steps.append(jnp.concatenate(
            [slab_ref[lb, pl.ds(s, SUBLANES, stride=SLAB_PITCH), :] for lb in range(D_MODEL // LANES)], axis=1))
    return jnp.concatenate(steps, axis=0)


def _from_segment_major(v, slab_ref):
    for s in range(SEG_STEPS):
        for lb in range(D_MODEL // LANES):
            slab_ref[lb, pl.ds(s, SUBLANES, stride=SLAB_PITCH), :] = (
                v[s * SUBLANES:(s + 1) * SUBLANES, lb * LANES:(lb + 1) * LANES])
    cols = []
    for lb in range(D_MODEL // LANES):
        cols.append(jnp.concatenate(
            [slab_ref[lb, j * SLAB_PITCH:j * SLAB_PITCH + SEG_STEPS, :] for j in range(SUBLANES)], axis=0))
    return jnp.concatenate(cols, axis=1)


def _seg_conv(v, w, tails, bias=None):
    k = w.shape[0] - 1
    width = v.shape[1]
    sub = lax.broadcasted_iota(jnp.int32, (SUBLANES, width), 0)
    out = v * w[k:k + 1]
    if bias is not None:
        out = out + bias
    for back in range(1, k + 1):
        heads = []
        for r in range(back):
            s = SEG_STEPS - back + r
            wrapped = pltpu.roll(v[s * SUBLANES:(s + 1) * SUBLANES, :], 1, 0)
            heads.append(jnp.where(sub == 0, tails[k - back + r:k - back + r + 1, :], wrapped))
        shifted = jnp.concatenate(heads + [v[0:(SEG_STEPS - back) * SUBLANES, :]], axis=0)
        out = out + shifted * w[k - back:k - back + 1]
    return out


def _seg_tails(v, k):
    return jnp.concatenate(
        [v[(SEG_STEPS - q) * SUBLANES + SUBLANES - 1:(SEG_STEPS - q + 1) * SUBLANES, :] for q in range(k, 0, -1)],
        axis=0)


def _seg_scan(a, b, h0):
    def step(s, v):
        return v[s * SUBLANES:(s + 1) * SUBLANES, :]
    prod, loc = step(0, a), step(0, b)
    for s in range(1, SEG_STEPS):
        loc = step(s, a) * loc + step(s, b)
        prod = prod * step(s, a)
    h_in = h0
    entering = []
    for j in range(SUBLANES):
        entering.append(h_in)
        h_in = prod[j:j + 1] * h_in + loc[j:j + 1]
    cur = jnp.concatenate(entering, axis=0)
    outs = []
    for s in range(SEG_STEPS):
        cur = step(s, a) * cur + step(s, b)
        outs.append(cur)
    return jnp.concatenate(outs, axis=0), h_in


def _prompt_gla_branch(hb, wgla_ref, wlr_ref, gwa_ref, gba_ref, gng_ref, st_ref):
    tt = ROW_TILE
    la = _gla_log_decay(hb, wlr_ref, gwa_ref, gba_ref)
    row = lax.broadcasted_iota(jnp.int32, (tt, tt), 0)
    col = lax.broadcasted_iota(jnp.int32, (tt, tt), 1)
    chunk_bits = GLA_CHUNK.bit_length() - 1
    causal = jnp.logical_and(jnp.right_shift(row, chunk_bits) == jnp.right_shift(col, chunk_bits), col <= row)
    tri = jnp.where(causal, 1.0, 0.0).astype(BF16)
    la_hi = la.astype(BF16)
    la_lo = (la - la_hi.astype(F32)).astype(BF16)
    bcum = (jnp.dot(tri, la_hi, preferred_element_type=F32)
            + jnp.dot(tri, la_lo, preferred_element_type=F32))

    zq = _mm(hb, wgla_ref[:, 0:HDK]) * (GLA_DK ** -0.5)
    zk = _mm(hb, wgla_ref[:, HDK:2 * HDK])
    zv = _mm(hb, wgla_ref[:, 2 * HDK:2 * HDK + HDV])
    zg = _mm(hb, wgla_ref[:, 2 * HDK + HDV:2 * HDK + 2 * HDV])
    gng = gng_ref[...]
    n_chunk = tt // GLA_CHUNK
    gated = []
    for h in range(GLA_HEADS):
        ks = slice(h * GLA_DK, (h + 1) * GLA_DK)
        vs = slice(h * GLA_DV, (h + 1) * GLA_DV)
        b = bcum[:, ks]
        b3 = b.reshape(n_chunk, GLA_CHUNK, GLA_DK)
        b_mid = jnp.broadcast_to(b3[:, GLA_HALF - 1:GLA_HALF, :], b3.shape).reshape(tt, GLA_DK)
        b_end = jnp.broadcast_to(b3[:, GLA_CHUNK - 1:GLA_CHUNK, :], b3.shape).reshape(tt, GLA_DK)
        q = zq[:, ks]
        k = zk[:, ks]
        qe = q * jnp.exp(b - b_mid)
        ke = k * jnp.exp(b_mid - b)
        qs = (q * jnp.exp(b)).astype(BF16)
        kd = (k * jnp.exp(b_end - b)).astype(BF16)
        att = jnp.where(causal, _mm_nt(qe, ke), 0.0)
        vb = zv[:, vs].astype(BF16)
        o = jnp.dot(att.astype(BF16), vb, preferred_element_type=F32)
        st = st_ref[h]
        o_parts = []
        for c in range(n_chunk):
            rs = slice(c * GLA_CHUNK, (c + 1) * GLA_CHUNK)
            o_parts.append(o[rs] + _mm_nt(qs[rs], st))
            e_end = jnp.exp(b3[c, GLA_CHUNK - 1:GLA_CHUNK, :])
            st = st * e_end + _mm_tn(vb[rs], kd[rs])
        st_ref[h] = st
        gated.append(_rms(jnp.concatenate(o_parts, axis=0), gng) * _silu(zg[:, vs]))
    return gated


def _prompt_sc_branch(hb, wsc_ref, sccw_ref, sct_ref):
    d = D_MODEL
    u = _mm(hb, wsc_ref[:, d:2 * d]) * _mm(hb, wsc_ref[:, 2 * d:3 * d])
    uc = _seg_conv(u, sccw_ref[...], sct_ref[...])
    sct_ref[...] = _seg_tails(u, SC_CONV_W - 1)
    zb = _mm(hb, wsc_ref[:, 0:d])
    zgs = _mm(hb, wsc_ref[:, 3 * d:4 * d])
    return zb * uc * _silu(zgs)


def _prompt_rg_branch(hb, wrg_ref, rgcw_ref, rgcb_ref, rgwax_ref, rgba_ref, rgbx_ref, rglam_ref, rgwb_ref,
                      hcar_ref, rgt_ref, rgh_ref):
    d = D_MODEL
    n_cb = d // COL_BLOCK
    rw = rgcw_ref[...]
    old_tails = rgt_ref[...]

    def project(cb):
        return _mm(hb, wrg_ref[:, _col(cb)]), _mm(hb, wrg_ref[:, _col(cb, d)])

    def conv_and_gate_logits(cb, zx):
        cs = _col(cb)
        xc = _seg_conv(zx, rw[:, cs], old_tails[:, cs], rgcb_ref[:, cs])
        rgt_ref[:, cs] = _seg_tails(zx, RG_CONV_W - 1)
        return (xc,) + _rg_gate_logits(xc, cb, rgwax_ref)

    def recur(cb, xc, r_pre, i_pre, zgr):
        cs = _col(cb)
        a, bt = _rg_coeffs(xc, r_pre, i_pre, cb, rgba_ref, rgbx_ref, rglam_ref)
        h_all, h_last = _seg_scan(a, bt, hcar_ref[:, cs])
        hcar_ref[:, cs] = h_last
        rgh_ref[:, cs] = h_last
        return h_all * _silu(zgr)

    proj = {0: project(0), 1: project(1)}
    logits = {0: conv_and_gate_logits(0, proj[0][0])}
    y = None
    for cb in range(n_cb):
        if cb + 2 < n_cb:
            proj[cb + 2] = project(cb + 2)
        if cb + 1 < n_cb:
            logits[cb + 1] = conv_and_gate_logits(cb + 1, proj[cb + 1][0])
        t = recur(cb, *logits[cb], proj[cb][1])
        y = _acc(y, _mm(t, rgwb_ref[_col(cb), :]))
    return y


def _const_spec(shape):
    nd = len(shape)
    return pl.BlockSpec(shape, lambda b, i: (0,) * nd, pipeline_mode=pl.Buffered(1))


def _prompt_layer(x, mod, p, final_gain, final):
    bsz, seq, d = x.shape
    tt = ROW_TILE
    weights = (p["ng"], p["wgla"], p["wlr"], p["gwa"], p["gba"], p["gng"], p["gwb"],
               p["wsc"], p["sccw"], p["scwb"], p["wrg"], p["rgcw"], p["rgcb"], p["rgwax"], p["rgba"],
               p["rgbx"], p["rglam"], p["rgwb"], p["wm"], p["bm"], p["wout"], final_gain)
    in_specs = [pl.BlockSpec((None, tt, d), lambda b, i: (b, i, 0)),
                pl.BlockSpec((None, 1, 3 * d), lambda b, i: (b, 0, 0))]
    in_specs += [_const_spec(w.shape) for w in weights]
    out_specs = [pl.BlockSpec((None, tt, d), lambda b, i: (b, i, 0)),
                 pl.BlockSpec((None, GLA_HEADS, GLA_DK, GLA_DV), lambda b, i: (b, 0, 0, 0)),
                 pl.BlockSpec((None, SC_CONV_W - 1, d), lambda b, i: (b, 0, 0)),
                 pl.BlockSpec((None, RG_CONV_W - 1, d), lambda b, i: (b, 0, 0)),
                 pl.BlockSpec((None, 1, d), lambda b, i: (b, 0, 0))]
    out_shape = [jax.ShapeDtypeStruct((bsz, seq, d), F32),
                 jax.ShapeDtypeStruct((bsz, GLA_HEADS, GLA_DK, GLA_DV), F32),
                 jax.ShapeDtypeStruct((bsz, SC_CONV_W - 1, d), F32),
                 jax.ShapeDtypeStruct((bsz, RG_CONV_W - 1, d), F32),
                 jax.ShapeDtypeStruct((bsz, 1, d), F32)]
    scratch = [pltpu.VMEM((GLA_HEADS, GLA_DV, GLA_DK), F32),
               pltpu.VMEM((1, d), F32)]
    scratch += [pltpu.VMEM((d // LANES, SUBLANES * SLAB_PITCH, LANES), F32)] * 3
    return pl.pallas_call(
        functools.partial(_prompt_layer_body, final=final),
        grid=(bsz, seq // tt),
        in_specs=in_specs,
        out_specs=out_specs,
        out_shape=out_shape,
        scratch_shapes=scratch,
        compiler_params=pltpu.CompilerParams(
            dimension_semantics=("arbitrary", "arbitrary"), vmem_limit_bytes=VMEM_LIMIT),
        name="prompt_layer",
    )(x, mod, *weights)


def _sample_gla_prep_body(x_ref, mod_ref, ng_ref, wgla_ref, wlr_ref, gwa_ref, gba_ref,
                          q_ref, k_ref, v_ref, ea_ref):
    hb = _mod_norm(x_ref[...], mod_ref[...], ng_ref[...]).astype(BF16)
    la = _gla_log_decay(hb, wlr_ref, gwa_ref, gba_ref)
    ea_ref[...] = jnp.exp(la)
    q_ref[...] = _mm(hb, wgla_ref[:, 0:HDK]) * (GLA_DK ** -0.5)
    k_ref[...] = _mm(hb, wgla_ref[:, HDK:2 * HDK])
    v_ref[...] = _mm(hb, wgla_ref[:, 2 * HDK:2 * HDK + HDV])


def _sample_gla_prep(x, mod, p):
    rows = x.shape[0]
    return pl.pallas_call(
        _sample_gla_prep_body,
        out_shape=[jax.ShapeDtypeStruct((rows, HDK), F32),
                   jax.ShapeDtypeStruct((rows, HDK), F32),
                   jax.ShapeDtypeStruct((rows, HDV), F32),
                   jax.ShapeDtypeStruct((rows, HDK), F32)],
        compiler_params=pltpu.CompilerParams(vmem_limit_bytes=VMEM_LIMIT),
        name="sample_gla_prep",
    )(x, mod, p["ng"], p["wgla"], p["wlr"], p["gwa"], p["gba"])


def _sample_gla_state_body(q_ref, k_ref, ea_ref, v_ref, s_ref, *rest):
    o_ref, so_ref = rest[-2:]
    for h in range(GLA_HEADS):
        ks = slice(h * GLA_DK, (h + 1) * GLA_DK)
        vs = slice(h * GLA_DV, (h + 1) * GLA_DV)
        q_t = q_ref[:, ks].T
        k_t = k_ref[:, ks].T
        e_t = ea_ref[:, ks].T
        for j in range(STATE_BATCH_TILE):
            s_new = e_t[:, j:j + 1] * s_ref[j, h] + k_t[:, j:j + 1] * v_ref[j:j + 1, vs]
            so_ref[j, h] = s_new
            o_ref[j:j + 1, vs] = jnp.sum(q_t[:, j:j + 1] * s_new, axis=0, keepdims=True)


def _sample_gla_state(q, k, ea, v, state, layer, carrier):
    rows = q.shape[0]
    bt = STATE_BATCH_TILE
    state_block = (None, bt, GLA_HEADS, GLA_DK, GLA_DV)
    state_index = lambda i: (layer, i, 0, 0, 0)
    args = [q, k, ea, v, state]
    in_specs = [pl.BlockSpec((bt, HDK), lambda i: (i, 0)),
                pl.BlockSpec((bt, HDK), lambda i: (i, 0)),
                pl.BlockSpec((bt, HDK), lambda i: (i, 0)),
                pl.BlockSpec((bt, HDV), lambda i: (i, 0)),
                pl.BlockSpec(state_block, state_index)]
    aliases = {}
    if carrier is not None:
        args.append(carrier)
        in_specs.append(pl.BlockSpec(memory_space=pl.ANY))
        aliases = {len(args) - 1: 1}
    return pl.pallas_call(
        _sample_gla_state_body,
        grid=(rows // bt,),
        in_specs=in_specs,
        out_specs=[pl.BlockSpec((bt, HDV), lambda i: (i, 0)),
                   pl.BlockSpec(state_block, state_index)],
        out_shape=[jax.ShapeDtypeStruct((rows, HDV), F32),
                   jax.ShapeDtypeStruct(state.shape, F32)],
        input_output_aliases=aliases,
        name="sample_gla_state",
    )(*args)


def _sample_tail_body(x_ref, mod_ref, o_ref, ssc_ref, srg_ref, srh_ref,
                      ng_ref, wgla_ref, gng_ref, gwb_ref, wsc_ref, sccw_ref, scwb_ref, wrg_ref, rgcw_ref,
                      rgcb_ref, rgwax_ref, rgba_ref, rgbx_ref, rglam_ref, rgwb_ref, wm_ref, bm_ref, wout_ref,
                      fg_ref, y_ref, nsc_ref, nrg_ref, nrh_ref, *, final):
    d = D_MODEL
    x = x_ref[...]
    mod = mod_ref[...]
    gate = mod[:, 2 * d:3 * d]
    hb = _mod_norm(x, mod, ng_ref[...]).astype(BF16)

    zg = _mm(hb, wgla_ref[:, 2 * HDK + HDV:2 * HDK + 2 * HDV])
    parts = _gla_out_norm_gate(o_ref[...], zg, gng_ref[...])
    y_gla = _mm(jnp.concatenate(parts, axis=1), gwb_ref[...])

    u = _mm(hb, wsc_ref[:, d:2 * d]) * _mm(hb, wsc_ref[:, 2 * d:3 * d])
    cw = sccw_ref[...]
    s0, s1 = ssc_ref[:, 0:d], ssc_ref[:, d:2 * d]
    uc = s0 * cw[0:1] + s1 * cw[1:2] + u * cw[2:3]
    nsc_ref[:, 0:d] = s1
    nsc_ref[:, d:2 * d] = u
    zb = _mm(hb, wsc_ref[:, 0:d])
    zgs = _mm(hb, wsc_ref[:, 3 * d:4 * d])
    y_sc = _mm(zb * uc * _silu(zgs), scwb_ref[...])

    zx = _mm(hb, wrg_ref[:, 0:d])
    rw = rgcw_ref[...]
    r0, r1, r2 = srg_ref[:, 0:d], srg_ref[:, d:2 * d], srg_ref[:, 2 * d:3 * d]
    xc = r0 * rw[0:1] + r1 * rw[1:2] + r2 * rw[2:3] + zx * rw[3:4] + rgcb_ref[...]
    nrg_ref[:, 0:d] = r1
    nrg_ref[:, d:2 * d] = r2
    nrg_ref[:, 2 * d:3 * d] = zx
    h_parts = []
    for cb in range(d // COL_BLOCK):
        cs = _col(cb)
        r_pre, i_pre = _rg_gate_logits(xc[:, cs], cb, rgwax_ref)
        a, bt = _rg_coeffs(xc[:, cs], r_pre, i_pre, cb, rgba_ref, rgbx_ref, rglam_ref)
        h_parts.append(a * srh_ref[:, cs] + bt)
    hr = jnp.concatenate(h_parts, axis=1)
    nrh_ref[...] = hr
    zgr = _mm(hb, wrg_ref[:, d:2 * d])
    y_rg = _mm(hr * _silu(zgr), rgwb_ref[...])

    logits = tuple(_mm(hb, wm_ref[:, n * d:(n + 1) * d]) for n in range(3))
    xn = _merge_out(x, gate, (y_gla, y_sc, y_rg), logits, bm_ref, wout_ref)
    if final:
        xn = _rms(xn, fg_ref[...])
    y_ref[...] = xn


def _sample_tail(x, mod, o, ssc, srg, srh, p, final_gain, final):
    rows, d = x.shape
    return pl.pallas_call(
        functools.partial(_sample_tail_body, final=final),
        out_shape=[jax.ShapeDtypeStruct((rows, d), F32),
                   jax.ShapeDtypeStruct((rows, (SC_CONV_W - 1) * d), F32),
                   jax.ShapeDtypeStruct((rows, (RG_CONV_W - 1) * d), F32),
                   jax.ShapeDtypeStruct((rows, d), F32)],
        compiler_params=pltpu.CompilerParams(vmem_limit_bytes=VMEM_LIMIT),
        name="sample_tail",
    )(x, mod, o, ssc, srg, srh,
      p["ng"], p["wgla"], p["gng"], p["gwb"], p["wsc"], p["sccw"], p["scwb"], p["wrg"], p["rgcw"],
      p["rgcb"], p["rgwax"], p["rgba"], p["rgbx"], p["rglam"], p["rgwb"], p["wm"], p["bm"], p["wout"],
      final_gain)


def _layer_params(l, norm_gain, w_in, gla_w_alpha, gla_b_alpha, gla_norm_gain, gla_w_branch, sc_conv_w,
                  sc_w_branch, rg_conv_w, rg_conv_b, rg_w_a, rg_b_a, rg_w_x, rg_b_x, rg_lambda,
                  rg_w_branch, b_merge, w_out):
    d = D_MODEL
    w = w_in[l]
    wlr = jnp.zeros((d, LANES), BF16).at[:, 0:GLA_RANK].set(w[:, OFF_LR:OFF_SC].astype(BF16))
    gwa = jnp.zeros((LANES, HDK), BF16).at[0:GLA_RANK, :].set(gla_w_alpha[l].astype(BF16))
    return {
        "ng": norm_gain[l].reshape(1, d),
        "wgla": w[:, OFF_GLA:OFF_LR].astype(BF16),
        "wlr": wlr,
        "gwa": gwa,
        "gba": gla_b_alpha[l].reshape(1, HDK),
        "gng": gla_norm_gain[l].reshape(1, GLA_DV),
        "gwb": gla_w_branch[l].astype(BF16),
        "wsc": w[:, OFF_SC:OFF_RG].astype(BF16),
        "sccw": sc_conv_w[l],
        "scwb": sc_w_branch[l].astype(BF16),
        "wrg": w[:, OFF_RG:OFF_MERGE].astype(BF16),
        "rgcw": rg_conv_w[l],
        "rgcb": rg_conv_b[l].reshape(1, d),
        "rgwax": jnp.concatenate([rg_w_a[l], rg_w_x[l]], axis=-1).astype(BF16),
        "rgba": rg_b_a[l].reshape(1, d),
        "rgbx": rg_b_x[l].reshape(1, d),
        "rglam": rg_lambda[l].reshape(1, d),
        "rgwb": rg_w_branch[l].astype(BF16),
        "wm": w[:, OFF_MERGE:N_IN].astype(BF16),
        "bm": b_merge[l].reshape(1, 3 * d),
        "wout": w_out[l].astype(BF16),
    }


def kernel(x_prompt, x_sample, c_prompt, c_sample, state_gla, state_sc_conv, state_rg_conv, state_rg_h, w_ada, b_ada, norm_gain, w_in, gla_w_alpha, gla_b_alpha, gla_norm_gain, gla_w_branch, sc_conv_w, sc_w_branch, rg_conv_w, rg_conv_b, rg_w_a, rg_b_a, rg_w_x, rg_b_x, rg_lambda, rg_w_branch, b_merge, w_out, final_gain):
    d = D_MODEL
    bp = x_prompt.shape[0]
    bs = x_sample.shape[0]
    fg = final_gain.reshape(1, d)
    mod_all = _ada_mod(jnp.concatenate([c_prompt, c_sample], axis=0), w_ada, b_ada)

    xp = x_prompt
    xs = x_sample.reshape(bs, d)
    gla_p, sc_p, rgc_p, rgh_p = [], [], [], []
    gla_s, sc_s, rgc_s, rgh_s = None, [], [], []
    for l in range(DEPTH):
        p = _layer_params(l, norm_gain, w_in, gla_w_alpha, gla_b_alpha, gla_norm_gain, gla_w_branch,
                          sc_conv_w, sc_w_branch, rg_conv_w, rg_conv_b, rg_w_a, rg_b_a, rg_w_x, rg_b_x,
                          rg_lambda, rg_w_branch, b_merge, w_out)
        final = l == DEPTH - 1
        mod_p = mod_all[l, 0:bp].reshape(bp, 1, 3 * d)
        mod_s = mod_all[l, bp:bp + bs]

        xp, g, sc, rc, rh = _prompt_layer(xp, mod_p, p, fg, final)
        gla_p.append(g)
        sc_p.append(sc)
        rgc_p.append(rc)
        rgh_p.append(rh.reshape(bp, d))

        q, k, v, ea = _sample_gla_prep(xs, mod_s, p)
        o, gla_s = _sample_gla_state(q, k, ea, v, state_gla, l, gla_s)
        xs, nsc, nrg, nrh = _sample_tail(xs, mod_s, o,
                                         state_sc_conv[l].reshape(bs, (SC_CONV_W - 1) * d),
                                         state_rg_conv[l].reshape(bs, (RG_CONV_W - 1) * d),
                                         state_rg_h[l], p, fg, final)
        sc_s.append(nsc.reshape(bs, SC_CONV_W - 1, d))
        rgc_s.append(nrg.reshape(bs, RG_CONV_W - 1, d))
        rgh_s.append(nrh)

    return (xp, xs.reshape(bs, 1, d),
            jnp.stack(gla_p), jnp.stack(sc_p), jnp.stack(rgc_p), jnp.stack(rgh_p),
            gla_s, jnp.stack(sc_s), jnp.stack(rgc_s), jnp.stack(rgh_s))
```

```python
import functools

import jax
import jax.numpy as jnp
from jax import lax
from jax.experimental import pallas as pl
from jax.experimental.pallas import tpu as pltpu

F32 = jnp.float32
BF16 = jnp.bfloat16

D_MODEL = 1024
DEPTH = 2
GLA_HEADS = 4
GLA_DK = 128
GLA_DV = 256
GLA_RANK = 16
GLA_GATE_NORM = 16.0
HDK = GLA_HEADS * GLA_DK
HDV = GLA_HEADS * GLA_DV
SC_CONV_W = 3
RG_CONV_W = 4
RG_BLOCKS = 8
RG_BLOCK = 128
RG_C = 8.0
RMS_EPS = 1e-6

LANES = 128
SUBLANES = 8
ROW_TILE = 256
GLA_CHUNK = 64
GLA_HALF = GLA_CHUNK // 2
COL_BLOCK = 256
SEG_STEPS = ROW_TILE // SUBLANES
SLAB_PITCH = SEG_STEPS + 4
STATE_BATCH_TILE = 8
VMEM_LIMIT = 60 * 1024 * 1024

OFF_GLA = 0
OFF_LR = OFF_GLA + 2 * HDK + 2 * HDV
OFF_SC = OFF_LR + GLA_RANK
OFF_RG = OFF_SC + 4 * D_MODEL
OFF_MERGE = OFF_RG + 2 * D_MODEL
N_IN = OFF_MERGE + 3 * D_MODEL


def _mm(a, w):
    return jnp.dot(a.astype(BF16), w, preferred_element_type=F32)


def _mm_nt(a, b):
    return lax.dot_general(a.astype(BF16), b.astype(BF16), (((1,), (1,)), ((), ())),
                           preferred_element_type=F32)


def _mm_tn(a, b):
    return lax.dot_general(a.astype(BF16), b.astype(BF16), (((0,), (0,)), ((), ())),
                           preferred_element_type=F32)


def _rms(x, g):
    ms = jnp.mean(x * x, axis=-1, keepdims=True)
    return x * lax.rsqrt(ms + RMS_EPS) * g


def _silu(x):
    return x * jax.nn.sigmoid(x)


def _mod_norm(x, mod, g):
    d = D_MODEL
    shift, scale = mod[:, 0:d], mod[:, d:2 * d]
    return _rms(x, g) * (1.0 + scale) + shift


def _gla_log_decay(hb, wlr_ref, gwa_ref, gba_ref):
    zlr = _mm(hb, wlr_ref[...])
    pre = _mm(zlr, gwa_ref[...]) + gba_ref[...]
    return jax.nn.log_sigmoid(pre) * (1.0 / GLA_GATE_NORM)


def _gla_out_norm_gate(o, zg, gng):
    parts = []
    for h in range(GLA_HEADS):
        sl = slice(h * GLA_DV, (h + 1) * GLA_DV)
        parts.append(_rms(o[:, sl], gng) * _silu(zg[:, sl]))
    return parts


def _col(cb, base=0):
    return slice(base + cb * COL_BLOCK, base + (cb + 1) * COL_BLOCK)


def _acc(total, part):
    return part if total is None else total + part


def _rg_gate_logits(xc, cb, rgwax_ref):
    xb = xc.astype(BF16)
    per_cb = COL_BLOCK // RG_BLOCK
    r_parts, i_parts = [], []
    for n in range(per_cb):
        ri = jnp.dot(xb[:, n * RG_BLOCK:(n + 1) * RG_BLOCK], rgwax_ref[cb * per_cb + n],
                     preferred_element_type=F32)
        r_parts.append(ri[:, 0:RG_BLOCK])
        i_parts.append(ri[:, RG_BLOCK:2 * RG_BLOCK])
    return jnp.concatenate(r_parts, axis=1), jnp.concatenate(i_parts, axis=1)


def _rg_coeffs(xc, r_pre, i_pre, cb, ba_ref, bx_ref, lam_ref):
    cs = _col(cb)
    r = jax.nn.sigmoid(r_pre + ba_ref[:, cs])
    ig = jax.nn.sigmoid(i_pre + bx_ref[:, cs])
    lam = lam_ref[:, cs]
    softplus_neg_lam = jnp.log1p(jnp.exp(-jnp.abs(lam))) + jnp.maximum(-lam, 0.0)
    log_a = (-RG_C) * r * softplus_neg_lam
    a = jnp.exp(log_a)
    rad = 1.0 - a * a
    root = jnp.where(rad > 0.0, rad * lax.rsqrt(rad), 0.0)
    bterm = root * (ig * xc)
    return a, bterm


def _merge_out(x, gate, ys, logits, bm_ref, wout_ref):
    d = D_MODEL
    mix = None
    for n, (y, m) in enumerate(zip(ys, logits)):
        mix = _acc(mix, jax.nn.sigmoid(m + bm_ref[:, n * d:(n + 1) * d]) * y)
    return x + gate * _mm(mix, wout_ref[...])


def _ada_body(c_ref, w_ref, b_ref, o_ref):
    o_ref[...] = _mm(_silu(c_ref[...]), w_ref[...].astype(BF16)) + b_ref[...]


def _ada_mod(c_all, w_ada, b_ada):
    rows = c_all.shape[0]
    d = D_MODEL
    return pl.pallas_call(
        _ada_body,
        grid=(DEPTH, 3),
        in_specs=[pl.BlockSpec((rows, d), lambda l, j: (0, 0)),
                  pl.BlockSpec((None, d, d), lambda l, j: (l, 0, j)),
                  pl.BlockSpec((None, 1, d), lambda l, j: (l, 0, j))],
        out_specs=pl.BlockSpec((None, rows, d), lambda l, j: (l, 0, j)),
        out_shape=jax.ShapeDtypeStruct((DEPTH, rows, 3 * d), F32),
        name="ada_mod",
    )(c_all, w_ada, b_ada.reshape(DEPTH, 1, 3 * d))


def _prompt_layer_body(x_ref, mod_ref, ng_ref, wgla_ref, wlr_ref, gwa_ref, gba_ref, gng_ref, gwb_ref,
                       wsc_ref, sccw_ref, scwb_ref, wrg_ref, rgcw_ref, rgcb_ref, rgwax_ref, rgba_ref,
                       rgbx_ref, rglam_ref, rgwb_ref, wm_ref, bm_ref, wout_ref, fg_ref,
                       y_ref, gla_ref, sct_ref, rgt_ref, rgh_ref,
                       st_ref, hcar_ref, slab_h_ref, slab_g_ref, slab_o_ref, *, final):
    d = D_MODEL
    i = pl.program_id(1)

    @pl.when(i == 0)
    def _():
        st_ref[...] = jnp.zeros(st_ref.shape, F32)
        sct_ref[...] = jnp.zeros(sct_ref.shape, F32)
        rgt_ref[...] = jnp.zeros(rgt_ref.shape, F32)
        hcar_ref[...] = jnp.zeros(hcar_ref.shape, F32)

    x = x_ref[...]
    mod = mod_ref[...]
    gate = mod[:, 2 * d:3 * d]
    h = _mod_norm(x, mod, ng_ref[...])
    hb = h.astype(BF16)
    hb_s = _to_segment_major(h, slab_h_ref).astype(BF16)

    y_rg = _from_segment_major(
        _prompt_rg_branch(hb_s, wrg_ref, rgcw_ref, rgcb_ref, rgwax_ref, rgba_ref, rgbx_ref, rglam_ref,
                          rgwb_ref, hcar_ref, rgt_ref, rgh_ref), slab_o_ref)
    t_gla = _prompt_gla_branch(hb, wgla_ref, wlr_ref, gwa_ref, gba_ref, gng_ref, st_ref)
    y_gla = None
    for hd in range(GLA_HEADS):
        y_gla = _acc(y_gla, _mm(t_gla[hd], gwb_ref[hd * GLA_DV:(hd + 1) * GLA_DV, :]))
    y_sc = _from_segment_major(
        _mm(_prompt_sc_branch(hb_s, wsc_ref, sccw_ref, sct_ref), scwb_ref[...]), slab_g_ref)
    logits = tuple(_mm(hb, wm_ref[:, n * d:(n + 1) * d]) for n in range(3))
    xn = _merge_out(x, gate, (y_gla, y_sc, y_rg), logits, bm_ref, wout_ref)
    if final:
        xn = _rms(xn, fg_ref[...])
    y_ref[...] = xn

    @pl.when(i == pl.num_programs(1) - 1)
    def _():
        for hd in range(GLA_HEADS):
            gla_ref[hd] = st_ref[hd].T


def _to_segment_major(v, slab_ref):
    for lb in range(D_MODEL // LANES):
        for j in range(SUBLANES):
            slab_ref[lb, j * SLAB_PITCH:j * SLAB_PITCH + SEG_STEPS, :] = (
                v[j * SEG_STEPS:(j + 1) * SEG_STEPS, lb * LANES:(lb + 1) * LANES])
    steps = []
    for s in range(SEG_STEPS):
        steps.append(jnp.concatenate(
            [slab_ref[lb, pl.ds(s, SUBLANES, stride=SLAB_PITCH), :] for lb in range(D_MODEL // LANES)], axis=1))
    return jnp.concatenate(steps, axis=0)


def _from_segment_major(v, slab_ref):
    for s in range(SEG_STEPS):
        for lb in range(D_MODEL // LANES):
            slab_ref[lb, pl.ds(s, SUBLANES, stride=SLAB_PITCH), :] = (
                v[s * SUBLANES:(s + 1) * SUBLANES, lb * LANES:(lb + 1) * LANES])
    cols = []
    for lb in range(D_MODEL // LANES):
        cols.append(jnp.concatenate(
            [slab_ref[lb, j * SLAB_PITCH:j * SLAB_PITCH + SEG_STEPS, :] for j in range(SUBLANES)], axis=0))
    return jnp.concatenate(cols, axis=1)


def _seg_conv(v, w, tails, bias=None):
    k = w.shape[0] - 1
    width = v.shape[1]
    sub = lax.broadcasted_iota(jnp.int32, (SUBLANES, width), 0)
    out = v * w[k:k + 1]
    if bias is not None:
        out = out + bias
    for back in range(1, k + 1):
        heads = []
        for r in range(back):
            s = SEG_STEPS - back + r
            wrapped = pltpu.roll(v[s * SUBLANES:(s + 1) * SUBLANES, :], 1, 0)
            heads.append(jnp.where(sub == 0, tails[k - back + r:k - back + r + 1, :], wrapped))
        shifted = jnp.concatenate(heads + [v[0:(SEG_STEPS - back) * SUBLANES, :]], axis=0)
        out = out + shifted * w[k - back:k - back + 1]
    return out


def _seg_tails(v, k):
    return jnp.concatenate(
        [v[(SEG_STEPS - q) * SUBLANES + SUBLANES - 1:(SEG_STEPS - q + 1) * SUBLANES, :] for q in range(k, 0, -1)],
        axis=0)


def _seg_scan(a, b, h0):
    def step(s, v):
        return v[s * SUBLANES:(s + 1) * SUBLANES, :]
    prod, loc = step(0, a), step(0, b)
    for s in range(1, SEG_STEPS):
        loc = step(s, a) * loc + step(s, b)
        prod = prod * step(s, a)
    h_in = h0
    entering = []
    for j in range(SUBLANES):
        entering.append(h_in)
        h_in = prod[j:j + 1] * h_in + loc[j:j + 1]
    cur = jnp.concatenate(entering, axis=0)
    outs = []
    for s in range(SEG_STEPS):
        cur = step(s, a) * cur + step(s, b)
        outs.append(cur)
    return jnp.concatenate(outs, axis=0), h_in


def _prompt_gla_branch(hb, wgla_ref, wlr_ref, gwa_ref, gba_ref, gng_ref, st_ref):
    tt = ROW_TILE
    zlr = _mm(hb, wlr_ref[...])
    zq = _mm(hb, wgla_ref[:, 0:HDK]) * (GLA_DK ** -0.5)
    pre = _mm(zlr, gwa_ref[...]) + gba_ref[...]
    zk = _mm(hb, wgla_ref[:, HDK:2 * HDK])
    la = jax.nn.log_sigmoid(pre) * (1.0 / GLA_GATE_NORM)
    row = lax.broadcasted_iota(jnp.int32, (tt, tt), 0)
    col = lax.broadcasted_iota(jnp.int32, (tt, tt), 1)
    chunk_bits = GLA_CHUNK.bit_length() - 1
    causal = jnp.logical_and(jnp.right_shift(row, chunk_bits) == jnp.right_shift(col, chunk_bits), col <= row)
    tri = jnp.where(causal, 1.0, 0.0).astype(BF16)
    la_hi = la.astype(BF16)
    la_lo = (la - la_hi.astype(F32)).astype(BF16)
    bcum = (jnp.dot(tri, la_hi, preferred_element_type=F32)
            + jnp.dot(tri, la_lo, preferred_element_type=F32))
    zv = _mm(hb, wgla_ref[:, 2 * HDK:2 * HDK + HDV])
    zg = _mm(hb, wgla_ref[:, 2 * HDK + HDV:2 * HDK + 2 * HDV])
    gng = gng_ref[...]
    n_chunk = tt // GLA_CHUNK
    gated = []
    for h in range(GLA_HEADS):
        ks = slice(h * GLA_DK, (h + 1) * GLA_DK)
        vs = slice(h * GLA_DV, (h + 1) * GLA_DV)
        b = bcum[:, ks]
        b3 = b.reshape(n_chunk, GLA_CHUNK, GLA_DK)
        b_mid = jnp.broadcast_to(b3[:, GLA_HALF - 1:GLA_HALF, :], b3.shape).reshape(tt, GLA_DK)
        b_end = jnp.broadcast_to(b3[:, GLA_CHUNK - 1:GLA_CHUNK, :], b3.shape).reshape(tt, GLA_DK)
        q = zq[:, ks]
        k = zk[:, ks]
        qe = q * jnp.exp(b - b_mid)
        ke = k * jnp.exp(b_mid - b)
        qs = (q * jnp.exp(b)).astype(BF16)
        kd = (k * jnp.exp(b_end - b)).astype(BF16)
        att = jnp.where(causal, _mm_nt(qe, ke), 0.0)
        vb = zv[:, vs].astype(BF16)
        o = jnp.dot(att.astype(BF16), vb, preferred_element_type=F32)
        st = st_ref[h]
        o_parts = []
        for c in range(n_chunk):
            rs = slice(c * GLA_CHUNK, (c + 1) * GLA_CHUNK)
            o_parts.append(o[rs] + _mm_nt(qs[rs], st))
            e_end = jnp.exp(b3[c, GLA_CHUNK - 1:GLA_CHUNK, :])
            st = st * e_end + _mm_tn(vb[rs], kd[rs])
        st_ref[h] = st
        gated.append(_rms(jnp.concatenate(o_parts, axis=0), gng) * _silu(zg[:, vs]))
    return gated


def _prompt_sc_branch(hb, wsc_ref, sccw_ref, sct_ref):
    d = D_MODEL
    u = _mm(hb, wsc_ref[:, d:2 * d]) * _mm(hb, wsc_ref[:, 2 * d:3 * d])
    uc = _seg_conv(u, sccw_ref[...], sct_ref[...])
    sct_ref[...] = _seg_tails(u, SC_CONV_W - 1)
    zb = _mm(hb, wsc_ref[:, 0:d])
    zgs = _mm(hb, wsc_ref[:, 3 * d:4 * d])
    return zb * uc * _silu(zgs)


def _prompt_rg_branch(hb, wrg_ref, rgcw_ref, rgcb_ref, rgwax_ref, rgba_ref, rgbx_ref, rglam_ref, rgwb_ref,
                      hcar_ref, rgt_ref, rgh_ref):
    d = D_MODEL
    n_cb = d // COL_BLOCK
    rw = rgcw_ref[...]
    old_tails = rgt_ref[...]

    def project(cb):
        return _mm(hb, wrg_ref[:, _col(cb)]), _mm(hb, wrg_ref[:, _col(cb, d)])

    def conv_and_gate_logits(cb, zx):
        cs = _col(cb)
        xc = _seg_conv(zx, rw[:, cs], old_tails[:, cs], rgcb_ref[:, cs])
        rgt_ref[:, cs] = _seg_tails(zx, RG_CONV_W - 1)
        return (xc,) + _rg_gate_logits(xc, cb, rgwax_ref)

    def recur(cb, xc, r_pre, i_pre, zgr):
        cs = _col(cb)
        a, bt = _rg_coeffs(xc, r_pre, i_pre, cb, rgba_ref, rgbx_ref, rglam_ref)
        h_all, h_last = _seg_scan(a, bt, hcar_ref[:, cs])
        hcar_ref[:, cs] = h_last
        rgh_ref[:, cs] = h_last
        return h_all * _silu(zgr)

    proj = {0: project(0), 1: project(1)}
    logits = {0: conv_and_gate_logits(0, proj[0][0])}
    y = None
    for cb in range(n_cb):
        if cb + 2 < n_cb:
            proj[cb + 2] = project(cb + 2)
        if cb + 1 < n_cb:
            logits[cb + 1] = conv_and_gate_logits(cb + 1, proj[cb + 1][0])
        t = recur(cb, *logits[cb], proj[cb][1])
        y = _acc(y, _mm(t, rgwb_ref[_col(cb), :]))
    return y


def _layer_spec(stacked, layer):
    rest = stacked.shape[1:]
    return pl.BlockSpec((None,) + rest, lambda *_: (layer,) + (0,) * len(rest), pipeline_mode=pl.Buffered(1))


def _whole_spec(arr):
    return pl.BlockSpec(arr.shape, lambda *_: (0,) * arr.ndim, pipeline_mode=pl.Buffered(1))


PROMPT_WEIGHTS = ("ng", "wgla", "wlr", "gwa", "gba", "gng", "gwb", "wsc", "sccw", "scwb", "wrg", "rgcw",
                  "rgcb", "rgwax", "rgba", "rgbx", "rglam", "rgwb", "wm", "bm", "wout")


def _prompt_layer(x, mod_all, mod_row0, p, final_gain, layer, final):
    bsz, seq, d = x.shape
    tt = ROW_TILE
    weights = tuple(p[name] for name in PROMPT_WEIGHTS)
    mod4 = mod_all.reshape(DEPTH, mod_all.shape[1], 1, 3 * d)
    in_specs = [pl.BlockSpec((None, tt, d), lambda b, i: (b, i, 0)),
                pl.BlockSpec((None, None, 1, 3 * d), lambda b, i: (layer, mod_row0 + b, 0, 0))]
    in_specs += [_layer_spec(w, layer) for w in weights] + [_whole_spec(final_gain)]
    out_specs = [pl.BlockSpec((None, tt, d), lambda b, i: (b, i, 0)),
                 pl.BlockSpec((None, GLA_HEADS, GLA_DK, GLA_DV), lambda b, i: (b, 0, 0, 0)),
                 pl.BlockSpec((None, SC_CONV_W - 1, d), lambda b, i: (b, 0, 0)),
                 pl.BlockSpec((None, RG_CONV_W - 1, d), lambda b, i: (b, 0, 0)),
                 pl.BlockSpec((None, 1, d), lambda b, i: (b, 0, 0))]
    out_shape = [jax.ShapeDtypeStruct((bsz, seq, d), F32),
                 jax.ShapeDtypeStruct((bsz, GLA_HEADS, GLA_DK, GLA_DV), F32),
                 jax.ShapeDtypeStruct((bsz, SC_CONV_W - 1, d), F32),
                 jax.ShapeDtypeStruct((bsz, RG_CONV_W - 1, d), F32),
                 jax.ShapeDtypeStruct((bsz, 1, d), F32)]
    scratch = [pltpu.VMEM((GLA_HEADS, GLA_DV, GLA_DK), F32),
               pltpu.VMEM((1, d), F32)]
    scratch += [pltpu.VMEM((d // LANES, SUBLANES * SLAB_PITCH, LANES), F32)] * 3
    return pl.pallas_call(
        functools.partial(_prompt_layer_body, final=final),
        grid=(bsz, seq // tt),
        in_specs=in_specs,
        out_specs=out_specs,
        out_shape=out_shape,
        scratch_shapes=scratch,
        compiler_params=pltpu.CompilerParams(
            dimension_semantics=("arbitrary", "arbitrary"), vmem_limit_bytes=VMEM_LIMIT),
        name="prompt_layer",
    )(x, mod4, *weights, final_gain)


def _sample_gla_prep_body(x_ref, mod_ref, ng_ref, wgla_ref, wlr_ref, gwa_ref, gba_ref,
                          q_ref, k_ref, v_ref, ea_ref):
    hb = _mod_norm(x_ref[...], mod_ref[...], ng_ref[...]).astype(BF16)
    la = _gla_log_decay(hb, wlr_ref, gwa_ref, gba_ref)
    ea_ref[...] = jnp.exp(la)
    q_ref[...] = _mm(hb, wgla_ref[:, 0:HDK]) * (GLA_DK ** -0.5)
    k_ref[...] = _mm(hb, wgla_ref[:, HDK:2 * HDK])
    v_ref[...] = _mm(hb, wgla_ref[:, 2 * HDK:2 * HDK + HDV])


def _sample_mod_spec(rows, layer):
    return pl.BlockSpec((None, rows, 3 * D_MODEL), lambda *_: (layer, 0, 0))


def _sample_gla_prep(x, mod_all, p, layer):
    rows = x.shape[0]
    weights = tuple(p[name] for name in ("ng", "wgla", "wlr", "gwa", "gba"))
    out_shape = [jax.ShapeDtypeStruct((rows, HDK), F32),
                 jax.ShapeDtypeStruct((rows, HDK), F32),
                 jax.ShapeDtypeStruct((rows, HDV), F32),
                 jax.ShapeDtypeStruct((rows, HDK), F32)]
    return pl.pallas_call(
        _sample_gla_prep_body,
        grid=(1,),
        in_specs=[_whole_spec(x), _sample_mod_spec(rows, layer)] + [_layer_spec(w, layer) for w in weights],
        out_specs=[pl.BlockSpec(s.shape, lambda i: (0, 0)) for s in out_shape],
        out_shape=out_shape,
        compiler_params=pltpu.CompilerParams(vmem_limit_bytes=VMEM_LIMIT),
        name="sample_gla_prep",
    )(x, mod_all, *weights)


def _sample_gla_state_body(q_ref, k_ref, ea_ref, v_ref, s_ref, *rest):
    o_ref, so_ref = rest[-2:]

    @pl.when(pl.program_id(1) == 0)
    def _():
        for h in range(GLA_HEADS):
            ks = slice(h * GLA_DK, (h + 1) * GLA_DK)
            vs = slice(h * GLA_DV, (h + 1) * GLA_DV)
            q_t = q_ref[:, ks].T
            k_t = k_ref[:, ks].T
            e_t = ea_ref[:, ks].T
            for j in range(STATE_BATCH_TILE):
                s_new = e_t[:, j:j + 1] * s_ref[j, h] + k_t[:, j:j + 1] * v_ref[j:j + 1, vs]
                so_ref[j, h] = s_new
                o_ref[j:j + 1, vs] = jnp.sum(q_t[:, j:j + 1] * s_new, axis=0, keepdims=True)

    @pl.when(pl.program_id(1) == 1)
    def _():
        so_ref[...] = jnp.zeros(so_ref.shape, F32)


def _sample_gla_state(q, k, ea, v, state, layer, carrier):
    rows = q.shape[0]
    bt = STATE_BATCH_TILE
    state_block = (None, bt, GLA_HEADS, GLA_DK, GLA_DV)
    other = DEPTH - 1 - layer
    n_pass = 2 if carrier is None else 1
    args = [q, k, ea, v, state]
    in_specs = [pl.BlockSpec((bt, HDK), lambda i, p: (i, 0)),
                pl.BlockSpec((bt, HDK), lambda i, p: (i, 0)),
                pl.BlockSpec((bt, HDK), lambda i, p: (i, 0)),
                pl.BlockSpec((bt, HDV), lambda i, p: (i, 0)),
                pl.BlockSpec(state_block, lambda i, p: (layer, i, 0, 0, 0))]
    aliases = {}
    if carrier is not None:
        args.append(carrier)
        in_specs.append(pl.BlockSpec(memory_space=pl.ANY))
        aliases = {len(args) - 1: 1}
    return pl.pallas_call(
        _sample_gla_state_body,
        grid=(rows // bt, n_pass),
        in_specs=in_specs,
        out_specs=[pl.BlockSpec((bt, HDV), lambda i, p: (i, 0)),
                   pl.BlockSpec(state_block, lambda i, p: (layer + p * (other - layer), i, 0, 0, 0))],
        out_shape=[jax.ShapeDtypeStruct((rows, HDV), F32),
                   jax.ShapeDtypeStruct(state.shape, F32)],
        input_output_aliases=aliases,
        name="sample_gla_state",
    )(*args)


def _sample_tail_body(x_ref, mod_ref, o_ref, ssc_ref, srg_ref, srh_ref,
                      ng_ref, wgla_ref, gng_ref, gwb_ref, wsc_ref, sccw_ref, scwb_ref, wrg_ref, rgcw_ref,
                      rgcb_ref, rgwax_ref, rgba_ref, rgbx_ref, rglam_ref, rgwb_ref, wm_ref, bm_ref, wout_ref,
                      fg_ref, y_ref, nsc_ref, nrg_ref, nrh_ref, *, final):
    d = D_MODEL
    x = x_ref[...]
    mod = mod_ref[...]
    gate = mod[:, 2 * d:3 * d]
    hb = _mod_norm(x, mod, ng_ref[...]).astype(BF16)

    zg = _mm(hb, wgla_ref[:, 2 * HDK + HDV:2 * HDK + 2 * HDV])
    parts = _gla_out_norm_gate(o_ref[...], zg, gng_ref[...])
    y_gla = _mm(jnp.concatenate(parts, axis=1), gwb_ref[...])

    u = _mm(hb, wsc_ref[:, d:2 * d]) * _mm(hb, wsc_ref[:, 2 * d:3 * d])
    cw = sccw_ref[...]
    s0, s1 = ssc_ref[:, 0:d], ssc_ref[:, d:2 * d]
    uc = s0 * cw[0:1] + s1 * cw[1:2] + u * cw[2:3]
    nsc_ref[:, 0:d] = s1
    nsc_ref[:, d:2 * d] = u
    zb = _mm(hb, wsc_ref[:, 0:d])
    zgs = _mm(hb, wsc_ref[:, 3 * d:4 * d])
    y_sc = _mm(zb * uc * _silu(zgs), scwb_ref[...])

    zx = _mm(hb, wrg_ref[:, 0:d])
    rw = rgcw_ref[...]
    r0, r1, r2 = srg_ref[:, 0:d], srg_ref[:, d:2 * d], srg_ref[:, 2 * d:3 * d]
    xc = r0 * rw[0:1] + r1 * rw[1:2] + r2 * rw[2:3] + zx * rw[3:4] + rgcb_ref[...]
    nrg_ref[:, 0:d] = r1
    nrg_ref[:, d:2 * d] = r2
    nrg_ref[:, 2 * d:3 * d] = zx
    h_parts = []
    for cb in range(d // COL_BLOCK):
        cs = _col(cb)
        r_pre, i_pre = _rg_gate_logits(xc[:, cs], cb, rgwax_ref)
        a, bt = _rg_coeffs(xc[:, cs], r_pre, i_pre, cb, rgba_ref, rgbx_ref, rglam_ref)
        h_parts.append(a * srh_ref[:, cs] + bt)
    hr = jnp.concatenate(h_parts, axis=1)
    nrh_ref[...] = hr
    zgr = _mm(hb, wrg_ref[:, d:2 * d])
    y_rg = _mm(hr * _silu(zgr), rgwb_ref[...])

    logits = tuple(_mm(hb, wm_ref[:, n * d:(n + 1) * d]) for n in range(3))
    xn = _merge_out(x, gate, (y_gla, y_sc, y_rg), logits, bm_ref, wout_ref)
    if final:
        xn = _rms(xn, fg_ref[...])
    y_ref[...] = xn


def _sample_tail(x, mod_all, o, ssc, srg, srh, p, final_gain, layer, final):
    rows, d = x.shape
    weights = tuple(p[name] for name in ("ng", "wgla", "gng", "gwb", "wsc", "sccw", "scwb", "wrg", "rgcw",
                                         "rgcb", "rgwax", "rgba", "rgbx", "rglam", "rgwb", "wm", "bm", "wout"))
    out_shape = [jax.ShapeDtypeStruct((rows, d), F32),
                 jax.ShapeDtypeStruct((rows, (SC_CONV_W - 1) * d), F32),
                 jax.ShapeDtypeStruct((rows, (RG_CONV_W - 1) * d), F32),
                 jax.ShapeDtypeStruct((rows, d), F32)]
    in_specs = [_whole_spec(x), _sample_mod_spec(rows, layer), _whole_spec(o),
                _layer_spec(ssc, layer), _layer_spec(srg, layer), _layer_spec(srh, layer)]
    in_specs += [_layer_spec(w, layer) for w in weights] + [_whole_spec(final_gain)]
    return pl.pallas_call(
        functools.partial(_sample_tail_body, final=final),
        grid=(1,),
        in_specs=in_specs,
        out_specs=[pl.BlockSpec(s.shape, lambda i: (0, 0)) for s in out_shape],
        out_shape=out_shape,
        compiler_params=pltpu.CompilerParams(vmem_limit_bytes=VMEM_LIMIT),
        name="sample_tail",
    )(x, mod_all, o, ssc, srg, srh, *weights, final_gain)


def _stacked_params(norm_gain, w_in, gla_w_alpha, gla_b_alpha, gla_norm_gain, gla_w_branch, sc_conv_w,
                    sc_w_branch, rg_conv_w, rg_conv_b, rg_w_a, rg_b_a, rg_w_x, rg_b_x, rg_lambda,
                    rg_w_branch, b_merge, w_out):
    d = D_MODEL
    wb = w_in.astype(BF16)
    return {
        "ng": norm_gain.reshape(DEPTH, 1, d),
        "wgla": wb[:, :, OFF_GLA:OFF_LR],
        "wlr": jnp.pad(wb[:, :, OFF_LR:OFF_SC], ((0, 0), (0, 0), (0, LANES - GLA_RANK))),
        "gwa": jnp.pad(gla_w_alpha.astype(BF16), ((0, 0), (0, LANES - GLA_RANK), (0, 0))),
        "gba": gla_b_alpha.reshape(DEPTH, 1, HDK),
        "gng": gla_norm_gain.reshape(DEPTH, 1, GLA_DV),
        "gwb": gla_w_branch.astype(BF16),
        "wsc": wb[:, :, OFF_SC:OFF_RG],
        "sccw": sc_conv_w,
        "scwb": sc_w_branch.astype(BF16),
        "wrg": wb[:, :, OFF_RG:OFF_MERGE],
        "rgcw": rg_conv_w,
        "rgcb": rg_conv_b.reshape(DEPTH, 1, d),
        "rgwax": jnp.concatenate([rg_w_a, rg_w_x], axis=-1).astype(BF16),
        "rgba": rg_b_a.reshape(DEPTH, 1, d),
        "rgbx": rg_b_x.reshape(DEPTH, 1, d),
        "rglam": rg_lambda.reshape(DEPTH, 1, d),
        "rgwb": rg_w_branch.astype(BF16),
        "wm": wb[:, :, OFF_MERGE:N_IN],
        "bm": b_merge.reshape(DEPTH, 1, 3 * d),
        "wout": w_out.astype(BF16),
    }


def kernel(x_prompt, x_sample, c_prompt, c_sample, state_gla, state_sc_conv, state_rg_conv, state_rg_h, w_ada, b_ada, norm_gain, w_in, gla_w_alpha, gla_b_alpha, gla_norm_gain, gla_w_branch, sc_conv_w, sc_w_branch, rg_conv_w, rg_conv_b, rg_w_a, rg_b_a, rg_w_x, rg_b_x, rg_lambda, rg_w_branch, b_merge, w_out, final_gain):
    assert DEPTH == 2
    d = D_MODEL
    bp = x_prompt.shape[0]
    bs = x_sample.shape[0]
    fg = final_gain.reshape(1, d)
    mod_all = _ada_mod(jnp.concatenate([c_sample, c_prompt], axis=0), w_ada, b_ada)
    p = _stacked_params(norm_gain, w_in, gla_w_alpha, gla_b_alpha, gla_norm_gain, gla_w_branch,
                        sc_conv_w, sc_w_branch, rg_conv_w, rg_conv_b, rg_w_a, rg_b_a, rg_w_x, rg_b_x,
                        rg_lambda, rg_w_branch, b_merge, w_out)
    ssc = state_sc_conv.reshape(DEPTH, bs, (SC_CONV_W - 1) * d)
    srg = state_rg_conv.reshape(DEPTH, bs, (RG_CONV_W - 1) * d)

    xp = x_prompt
    xs = x_sample.reshape(bs, d)
    gla_p, sc_p, rgc_p, rgh_p = [], [], [], []
    gla_s, sc_s, rgc_s, rgh_s = None, [], [], []
    for l in range(DEPTH):
        final = l == DEPTH - 1
        xp, g, sc, rc, rh = _prompt_layer(xp, mod_all, bs, p, fg, l, final)
        gla_p.append(g)
        sc_p.append(sc)
        rgc_p.append(rc)
        rgh_p.append(rh.reshape(bp, d))

        q, k, v, ea = _sample_gla_prep(xs, mod_all, p, l)
        o, gla_s = _sample_gla_state(q, k, ea, v, state_gla, l, gla_s)
        xs, nsc, nrg, nrh = _sample_tail(xs, mod_all, o, ssc, srg, state_rg_h, p, fg, l, final)
        sc_s.append(nsc.reshape(bs, SC_CONV_W - 1, d))
        rgc_s.append(nrg.reshape(bs, RG_CONV_W - 1, d))
        rgh_s.append(nrh)

    return (xp, xs.reshape(bs, 1, d),
            jnp.stack(gla_p), jnp.stack(sc_p), jnp.stack(rgc_p), jnp.stack(rgh_p),
            gla_s, jnp.stack(sc_s), jnp.stack(rgc_s), jnp.stack(rgh_s))
```

```python
import functools

import jax
import jax.numpy as jnp
from jax import lax
from jax.experimental import pallas as pl
from jax.experimental.pallas import tpu as pltpu

F32 = jnp.float32
BF16 = jnp.bfloat16

D_MODEL = 1024
DEPTH = 2
GLA_HEADS = 4
GLA_DK = 128
GLA_DV = 256
GLA_RANK = 16
GLA_GATE_NORM = 16.0
HDK = GLA_HEADS * GLA_DK
HDV = GLA_HEADS * GLA_DV
SC_CONV_W = 3
RG_CONV_W = 4
RG_BLOCKS = 8
RG_BLOCK = 128
RG_C = 8.0
RMS_EPS = 1e-6

LANES = 128
SUBLANES = 8
ROW_TILE = 256
GLA_CHUNK = 64
GLA_HALF = GLA_CHUNK // 2
COL_BLOCK = 256
SEG_STEPS = ROW_TILE // SUBLANES
SLAB_PITCH = SEG_STEPS + 4
STATE_BATCH_TILE = 8
VMEM_LIMIT = 60 * 1024 * 1024

OFF_GLA = 0
OFF_LR = OFF_GLA + 2 * HDK + 2 * HDV
OFF_SC = OFF_LR + GLA_RANK
OFF_RG = OFF_SC + 4 * D_MODEL
OFF_MERGE = OFF_RG + 2 * D_MODEL
N_IN = OFF_MERGE + 3 * D_MODEL


def _mm(a, w):
    return jnp.dot(a.astype(BF16), w, preferred_element_type=F32)


def _mm_nt(a, b):
    return lax.dot_general(a.astype(BF16), b.astype(BF16), (((1,), (1,)), ((), ())),
                           preferred_element_type=F32)


def _mm_tn(a, b):
    return lax.dot_general(a.astype(BF16), b.astype(BF16), (((0,), (0,)), ((), ())),
                           preferred_element_type=F32)


def _rms(x, g):
    ms = jnp.mean(x * x, axis=-1, keepdims=True)
    return x * lax.rsqrt(ms + RMS_EPS) * g


def _silu(x):
    return x * jax.nn.sigmoid(x)


def _mod_norm(x, mod, g):
    d = D_MODEL
    shift, scale = mod[:, 0:d], mod[:, d:2 * d]
    return _rms(x, g) * (1.0 + scale) + shift


def _gla_log_decay(hb, wlr_ref, gwa_ref, gba_ref):
    zlr = _mm(hb, wlr_ref[...])
    pre = _mm(zlr, gwa_ref[...]) + gba_ref[...]
    return jax.nn.log_sigmoid(pre) * (1.0 / GLA_GATE_NORM)


def _gla_out_norm_gate(o, zg, gng):
    parts = []
    for h in range(GLA_HEADS):
        sl = slice(h * GLA_DV, (h + 1) * GLA_DV)
        parts.append(_rms(o[:, sl], gng) * _silu(zg[:, sl]))
    return parts


def _col(cb, base=0):
    return slice(base + cb * COL_BLOCK, base + (cb + 1) * COL_BLOCK)


def _acc(total, part):
    return part if total is None else total + part


def _rg_gate_logits(xc, cb, rgwax_ref):
    xb = xc.astype(BF16)
    per_cb = COL_BLOCK // RG_BLOCK
    r_parts, i_parts = [], []
    for n in range(per_cb):
        ri = jnp.dot(xb[:, n * RG_BLOCK:(n + 1) * RG_BLOCK], rgwax_ref[cb * per_cb + n],
                     preferred_element_type=F32)
        r_parts.append(ri[:, 0:RG_BLOCK])
        i_parts.append(ri[:, RG_BLOCK:2 * RG_BLOCK])
    return jnp.concatenate(r_parts, axis=1), jnp.concatenate(i_parts, axis=1)


def _rg_coeffs(xc, r_pre, i_pre, cb, ba_ref, bx_ref, lam_ref):
    cs = _col(cb)
    r = jax.nn.sigmoid(r_pre + ba_ref[:, cs])
    ig = jax.nn.sigmoid(i_pre + bx_ref[:, cs])
    lam = lam_ref[:, cs]
    softplus_neg_lam = jnp.log1p(jnp.exp(-jnp.abs(lam))) + jnp.maximum(-lam, 0.0)
    log_a = (-RG_C) * r * softplus_neg_lam
    a = jnp.exp(log_a)
    rad = 1.0 - a * a
    root = jnp.where(rad > 0.0, rad * lax.rsqrt(rad), 0.0)
    bterm = root * (ig * xc)
    return a, bterm


def _merge_out(x, gate, ys, logits, bm_ref, wout_ref):
    d = D_MODEL
    mix = None
    for n, (y, m) in enumerate(zip(ys, logits)):
        mix = _acc(mix, jax.nn.sigmoid(m + bm_ref[:, n * d:(n + 1) * d]) * y)
    return x + gate * _mm(mix, wout_ref[...])


def _ada_body(c_ref, w_ref, b_ref, o_ref):
    o_ref[...] = _mm(_silu(c_ref[...]), w_ref[...].astype(BF16)) + b_ref[...]


def _ada_mod(c_all, w_ada, b_ada):
    rows = c_all.shape[0]
    d = D_MODEL
    return pl.pallas_call(
        _ada_body,
        grid=(DEPTH, 3),
        in_specs=[pl.BlockSpec((rows, d), lambda l, j: (0, 0)),
                  pl.BlockSpec((None, d, d), lambda l, j: (l, 0, j)),
                  pl.BlockSpec((None, 1, d), lambda l, j: (l, 0, j))],
        out_specs=pl.BlockSpec((None, rows, d), lambda l, j: (l, 0, j)),
        out_shape=jax.ShapeDtypeStruct((DEPTH, rows, 3 * d), F32),
        name="ada_mod",
    )(c_all, w_ada, b_ada.reshape(DEPTH, 1, 3 * d))


W_IN_PIECES = (("wgla", OFF_GLA, OFF_LR), ("wsc", OFF_SC, OFF_RG), ("wrg", OFF_RG, OFF_MERGE),
               ("wm", OFF_MERGE, N_IN))
W_IN_ROW_BLOCK = 128


def _split_w_in_body(w_ref, wgla_ref, wsc_ref, wrg_ref, wm_ref, wlr_ref):
    for (_, lo, hi), o_ref in zip(W_IN_PIECES, (wgla_ref, wsc_ref, wrg_ref, wm_ref)):
        o_ref[...] = w_ref[:, lo:hi].astype(BF16)
    lane = lax.broadcasted_iota(jnp.int32, (w_ref.shape[0], LANES), 1)
    wlr_ref[...] = jnp.where(lane < GLA_RANK, w_ref[:, OFF_LR:OFF_LR + LANES], 0.0).astype(BF16)


def _split_w_in(w_in):
    d = D_MODEL
    rb = W_IN_ROW_BLOCK
    widths = [hi - lo for _, lo, hi in W_IN_PIECES] + [LANES]
    outs = pl.pallas_call(
        _split_w_in_body,
        grid=(DEPTH, d // rb),
        in_specs=[pl.BlockSpec((None, rb, N_IN), lambda l, i: (l, i, 0))],
        out_specs=[pl.BlockSpec((None, rb, n), lambda l, i: (l, i, 0)) for n in widths],
        out_shape=[jax.ShapeDtypeStruct((DEPTH, d, n), BF16) for n in widths],
        compiler_params=pltpu.CompilerParams(vmem_limit_bytes=VMEM_LIMIT),
        name="split_w_in",
    )(w_in)
    names = [name for name, _, _ in W_IN_PIECES] + ["wlr"]
    return dict(zip(names, outs))


def _prompt_layer_body(x_ref, mod_ref, ng_ref, wgla_ref, wlr_ref, gwa_ref, gba_ref, gng_ref, gwb_ref,
                       wsc_ref, sccw_ref, scwb_ref, wrg_ref, rgcw_ref, rgcb_ref, rgwax_ref, rgba_ref,
                       rgbx_ref, rglam_ref, rgwb_ref, wm_ref, bm_ref, wout_ref, fg_ref,
                       y_ref, gla_ref, sct_ref, rgt_ref, rgh_ref,
                       st_ref, hcar_ref, slab_h_ref, slab_g_ref, slab_o_ref, *, final):
    d = D_MODEL
    i = pl.program_id(1)

    @pl.when(i == 0)
    def _():
        st_ref[...] = jnp.zeros(st_ref.shape, F32)
        sct_ref[...] = jnp.zeros(sct_ref.shape, F32)
        rgt_ref[...] = jnp.zeros(rgt_ref.shape, F32)
        hcar_ref[...] = jnp.zeros(hcar_ref.shape, F32)

    x = x_ref[...]
    mod = mod_ref[...]
    gate = mod[:, 2 * d:3 * d]
    h = _mod_norm(x, mod, ng_ref[...])
    hb = h.astype(BF16)
    hb_s = _to_segment_major(h, slab_h_ref).astype(BF16)

    y_rg = _from_segment_major(
        _prompt_rg_branch(hb_s, wrg_ref, rgcw_ref, rgcb_ref, rgwax_ref, rgba_ref, rgbx_ref, rglam_ref,
                          rgwb_ref, hcar_ref, rgt_ref, rgh_ref), slab_o_ref)
    t_gla = _prompt_gla_branch(hb, wgla_ref, wlr_ref, gwa_ref, gba_ref, gng_ref, st_ref)
    y_gla = None
    for hd in range(GLA_HEADS):
        y_gla = _acc(y_gla, _mm(t_gla[hd], gwb_ref[hd * GLA_DV:(hd + 1) * GLA_DV, :]))
    y_sc = _from_segment_major(
        _mm(_prompt_sc_branch(hb_s, wsc_ref, sccw_ref, sct_ref), scwb_ref[...]), slab_g_ref)
    logits = tuple(_mm(hb, wm_ref[:, n * d:(n + 1) * d]) for n in range(3))
    xn = _merge_out(x, gate, (y_gla, y_sc, y_rg), logits, bm_ref, wout_ref)
    if final:
        xn = _rms(xn, fg_ref[...])
    y_ref[...] = xn

    @pl.when(i == pl.num_programs(1) - 1)
    def _():
        for hd in range(GLA_HEADS):
            gla_ref[hd] = st_ref[hd].T


def _to_segment_major(v, slab_ref):
    for lb in range(D_MODEL // LANES):
        for j in range(SUBLANES):
            slab_ref[lb, j * SLAB_PITCH:j * SLAB_PITCH + SEG_STEPS, :] = (
                v[j * SEG_STEPS:(j + 1) * SEG_STEPS, lb * LANES:(lb + 1) * LANES])
    steps = []
    for s in range(SEG_STEPS):
        steps.append(jnp.concatenate(
            [slab_ref[lb, pl.ds(s, SUBLANES, stride=SLAB_PITCH), :] for lb in range(D_MODEL // LANES)], axis=1))
    return jnp.concatenate(steps, axis=0)


def _from_segment_major(v, slab_ref):
    for s in range(SEG_STEPS):
        for lb in range(D_MODEL // LANES):
            slab_ref[lb, pl.ds(s, SUBLANES, stride=SLAB_PITCH), :] = (
                v[s * SUBLANES:(s + 1) * SUBLANES, lb * LANES:(lb + 1) * LANES])
    cols = []
    for lb in range(D_MODEL // LANES):
        cols.append(jnp.concatenate(
            [slab_ref[lb, j * SLAB_PITCH:j * SLAB_PITCH + SEG_STEPS, :] for j in range(SUBLANES)], axis=0))
    return jnp.concatenate(cols, axis=1)


def _seg_conv(v, w, tails, bias=None):
    k = w.shape[0] - 1
    width = v.shape[1]
    sub = lax.broadcasted_iota(jnp.int32, (SUBLANES, width), 0)
    out = v * w[k:k + 1]
    if bias is not None:
        out = out + bias
    for back in range(1, k + 1):
        heads = []
        for r in range(back):
            s = SEG_STEPS - back + r
            wrapped = pltpu.roll(v[s * SUBLANES:(s + 1) * SUBLANES, :], 1, 0)
            heads.append(jnp.where(sub == 0, tails[k - back + r:k - back + r + 1, :], wrapped))
        shifted = jnp.concatenate(heads + [v[0:(SEG_STEPS - back) * SUBLANES, :]], axis=0)
        out = out + shifted * w[k - back:k - back + 1]
    return out


def _seg_tails(v, k):
    return jnp.concatenate(
        [v[(SEG_STEPS - q) * SUBLANES + SUBLANES - 1:(SEG_STEPS - q + 1) * SUBLANES, :] for q in range(k, 0, -1)],
        axis=0)


def _seg_scan(a, b, h0):
    def step(s, v):
        return v[s * SUBLANES:(s + 1) * SUBLANES, :]
    prod, loc = step(0, a), step(0, b)
    for s in range(1, SEG_STEPS):
        loc = step(s, a) * loc + step(s, b)
        prod = prod * step(s, a)
    h_in = h0
    entering = []
    for j in range(SUBLANES):
        entering.append(h_in)
        h_in = prod[j:j + 1] * h_in + loc[j:j + 1]
    cur = jnp.concatenate(entering, axis=0)
    outs = []
    for s in range(SEG_STEPS):
        cur = step(s, a) * cur + step(s, b)
        outs.append(cur)
    return jnp.concatenate(outs, axis=0), h_in


def _prompt_gla_branch(hb, wgla_ref, wlr_ref, gwa_ref, gba_ref, gng_ref, st_ref):
    tt = ROW_TILE
    zlr = _mm(hb, wlr_ref[...])
    zq = _mm(hb, wgla_ref[:, 0:HDK]) * (GLA_DK ** -0.5)
    pre = _mm(zlr, gwa_ref[...]) + gba_ref[...]
    zk = _mm(hb, wgla_ref[:, HDK:2 * HDK])
    la = jax.nn.log_sigmoid(pre) * (1.0 / GLA_GATE_NORM)
    row = lax.broadcasted_iota(jnp.int32, (tt, tt), 0)
    col = lax.broadcasted_iota(jnp.int32, (tt, tt), 1)
    chunk_bits = GLA_CHUNK.bit_length() - 1
    causal = jnp.logical_and(jnp.right_shift(row, chunk_bits) == jnp.right_shift(col, chunk_bits), col <= row)
    tri = jnp.where(causal, 1.0, 0.0).astype(BF16)
    la_hi = la.astype(BF16)
    la_lo = (la - la_hi.astype(F32)).astype(BF16)
    bcum = (jnp.dot(tri, la_hi, preferred_element_type=F32)
            + jnp.dot(tri, la_lo, preferred_element_type=F32))
    zv = _mm(hb, wgla_ref[:, 2 * HDK:2 * HDK + HDV])
    zg = _mm(hb, wgla_ref[:, 2 * HDK + HDV:2 * HDK + 2 * HDV])
    gng = gng_ref[...]
    n_chunk = tt // GLA_CHUNK
    gated = []
    for h in range(GLA_HEADS):
        ks = slice(h * GLA_DK, (h + 1) * GLA_DK)
        vs = slice(h * GLA_DV, (h + 1) * GLA_DV)
        b = bcum[:, ks]
        b3 = b.reshape(n_chunk, GLA_CHUNK, GLA_DK)
        b_mid = jnp.broadcast_to(b3[:, GLA_HALF - 1:GLA_HALF, :], b3.shape).reshape(tt, GLA_DK)
        b_end = jnp.broadcast_to(b3[:, GLA_CHUNK - 1:GLA_CHUNK, :], b3.shape).reshape(tt, GLA_DK)
        q = zq[:, ks]
        k = zk[:, ks]
        qe = q * jnp.exp(b - b_mid)
        ke = k * jnp.exp(b_mid - b)
        qs = (q * jnp.exp(b)).astype(BF16)
        kd = (k * jnp.exp(b_end - b)).astype(BF16)
        att = jnp.where(causal, _mm_nt(qe, ke), 0.0)
        vb = zv[:, vs].astype(BF16)
        o = jnp.dot(att.astype(BF16), vb, preferred_element_type=F32)
        st = st_ref[h]
        o_parts = []
        for c in range(n_chunk):
            rs = slice(c * GLA_CHUNK, (c + 1) * GLA_CHUNK)
            o_parts.append(o[rs] + _mm_nt(qs[rs], st))
            e_end = jnp.exp(b3[c, GLA_CHUNK - 1:GLA_CHUNK, :])
            st = st * e_end + _mm_tn(vb[rs], kd[rs])
        st_ref[h] = st
        gated.append(_rms(jnp.concatenate(o_parts, axis=0), gng) * _silu(zg[:, vs]))
    return gated


def _prompt_sc_branch(hb, wsc_ref, sccw_ref, sct_ref):
    d = D_MODEL
    u = _mm(hb, wsc_ref[:, d:2 * d]) * _mm(hb, wsc_ref[:, 2 * d:3 * d])
    uc = _seg_conv(u, sccw_ref[...], sct_ref[...])
    sct_ref[...] = _seg_tails(u, SC_CONV_W - 1)
    zb = _mm(hb, wsc_ref[:, 0:d])
    zgs = _mm(hb, wsc_ref[:, 3 * d:4 * d])
    return zb * uc * _silu(zgs)


def _prompt_rg_branch(hb, wrg_ref, rgcw_ref, rgcb_ref, rgwax_ref, rgba_ref, rgbx_ref, rglam_ref, rgwb_ref,
                      hcar_ref, rgt_ref, rgh_ref):
    d = D_MODEL
    n_cb = d // COL_BLOCK
    rw = rgcw_ref[...]
    old_tails = rgt_ref[...]

    def project(cb):
        return _mm(hb, wrg_ref[:, _col(cb)]), _mm(hb, wrg_ref[:, _col(cb, d)])

    def conv_and_gate_logits(cb, zx):
        cs = _col(cb)
        xc = _seg_conv(zx, rw[:, cs], old_tails[:, cs], rgcb_ref[:, cs])
        rgt_ref[:, cs] = _seg_tails(zx, RG_CONV_W - 1)
        return (xc,) + _rg_gate_logits(xc, cb, rgwax_ref)

    def recur(cb, xc, r_pre, i_pre, zgr):
        cs = _col(cb)
        a, bt = _rg_coeffs(xc, r_pre, i_pre, cb, rgba_ref, rgbx_ref, rglam_ref)
        h_all, h_last = _seg_scan(a, bt, hcar_ref[:, cs])
        hcar_ref[:, cs] = h_last
        rgh_ref[:, cs] = h_last
        return h_all * _silu(zgr)

    proj = {0: project(0), 1: project(1)}
    logits = {0: conv_and_gate_logits(0, proj[0][0])}
    y = None
    for cb in range(n_cb):
        if cb + 2 < n_cb:
            proj[cb + 2] = project(cb + 2)
        if cb + 1 < n_cb:
            logits[cb + 1] = conv_and_gate_logits(cb + 1, proj[cb + 1][0])
        t = recur(cb, *logits[cb], proj[cb][1])
        y = _acc(y, _mm(t, rgwb_ref[_col(cb), :]))
    return y


def _layer_spec(stacked, layer):
    rest = stacked.shape[1:]
    return pl.BlockSpec((None,) + rest, lambda *_: (layer,) + (0,) * len(rest), pipeline_mode=pl.Buffered(1))


def _whole_spec(arr):
    return pl.BlockSpec(arr.shape, lambda *_: (0,) * arr.ndim, pipeline_mode=pl.Buffered(1))


PROMPT_WEIGHTS = ("ng", "wgla", "wlr", "gwa", "gba", "gng", "gwb", "wsc", "sccw", "scwb", "wrg", "rgcw",
                  "rgcb", "rgwax", "rgba", "rgbx", "rglam", "rgwb", "wm", "bm", "wout")


def _prompt_layer(x, mod_all, mod_row0, p, final_gain, layer, final):
    bsz, seq, d = x.shape
    tt = ROW_TILE
    weights = tuple(p[name] for name in PROMPT_WEIGHTS)
    mod4 = mod_all.reshape(DEPTH, mod_all.shape[1], 1, 3 * d)
    in_specs = [pl.BlockSpec((None, tt, d), lambda b, i: (b, i, 0)),
                pl.BlockSpec((None, None, 1, 3 * d), lambda b, i: (layer, mod_row0 + b, 0, 0))]
    in_specs += [_layer_spec(w, layer) for w in weights] + [_whole_spec(final_gain)]
    out_specs = [pl.BlockSpec((None, tt, d), lambda b, i: (b, i, 0)),
                 pl.BlockSpec((None, GLA_HEADS, GLA_DK, GLA_DV), lambda b, i: (b, 0, 0, 0)),
                 pl.BlockSpec((None, SC_CONV_W - 1, d), lambda b, i: (b, 0, 0)),
                 pl.BlockSpec((None, RG_CONV_W - 1, d), lambda b, i: (b, 0, 0)),
                 pl.BlockSpec((None, 1, d), lambda b, i: (b, 0, 0))]
    out_shape = [jax.ShapeDtypeStruct((bsz, seq, d), F32),
                 jax.ShapeDtypeStruct((bsz, GLA_HEADS, GLA_DK, GLA_DV), F32),
                 jax.ShapeDtypeStruct((bsz, SC_CONV_W - 1, d), F32),
                 jax.ShapeDtypeStruct((bsz, RG_CONV_W - 1, d), F32),
                 jax.ShapeDtypeStruct((bsz, 1, d), F32)]
    scratch = [pltpu.VMEM((GLA_HEADS, GLA_DV, GLA_DK), F32),
               pltpu.VMEM((1, d), F32)]
    scratch += [pltpu.VMEM((d // LANES, SUBLANES * SLAB_PITCH, LANES), F32)] * 3
    return pl.pallas_call(
        functools.partial(_prompt_layer_body, final=final),
        grid=(bsz, seq // tt),
        in_specs=in_specs,
        out_specs=out_specs,
        out_shape=out_shape,
        scratch_shapes=scratch,
        compiler_params=pltpu.CompilerParams(
            dimension_semantics=("arbitrary", "arbitrary"), vmem_limit_bytes=VMEM_LIMIT),
        name="prompt_layer",
    )(x, mod4, *weights, final_gain)


def _sample_gla_prep_body(x_ref, mod_ref, ng_ref, wgla_ref, wlr_ref, gwa_ref, gba_ref,
                          q_ref, k_ref, v_ref, ea_ref):
    hb = _mod_norm(x_ref[...], mod_ref[...], ng_ref[...]).astype(BF16)
    la = _gla_log_decay(hb, wlr_ref, gwa_ref, gba_ref)
    ea_ref[...] = jnp.exp(la)
    q_ref[...] = _mm(hb, wgla_ref[:, 0:HDK]) * (GLA_DK ** -0.5)
    k_ref[...] = _mm(hb, wgla_ref[:, HDK:2 * HDK])
    v_ref[...] = _mm(hb, wgla_ref[:, 2 * HDK:2 * HDK + HDV])


def _sample_mod_spec(rows, layer):
    return pl.BlockSpec((None, rows, 3 * D_MODEL), lambda *_: (layer, 0, 0))


def _sample_gla_prep(x, mod_all, p, layer):
    rows = x.shape[0]
    weights = tuple(p[name] for name in ("ng", "wgla", "wlr", "gwa", "gba"))
    out_shape = [jax.ShapeDtypeStruct((rows, HDK), F32),
                 jax.ShapeDtypeStruct((rows, HDK), F32),
                 jax.ShapeDtypeStruct((rows, HDV), F32),
                 jax.ShapeDtypeStruct((rows, HDK), F32)]
    return pl.pallas_call(
        _sample_gla_prep_body,
        grid=(1,),
        in_specs=[_whole_spec(x), _sample_mod_spec(rows, layer)] + [_layer_spec(w, layer) for w in weights],
        out_specs=[pl.BlockSpec(s.shape, lambda i: (0, 0)) for s in out_shape],
        out_shape=out_shape,
        compiler_params=pltpu.CompilerParams(vmem_limit_bytes=VMEM_LIMIT),
        name="sample_gla_prep",
    )(x, mod_all, *weights)


def _sample_gla_state_body(q_ref, k_ref, ea_ref, v_ref, s_ref, *rest):
    o_ref, so_ref = rest[-2:]
    for h in range(GLA_HEADS):
        ks = slice(h * GLA_DK, (h + 1) * GLA_DK)
        vs = slice(h * GLA_DV, (h + 1) * GLA_DV)
        q_t = q_ref[:, ks].T
        k_t = k_ref[:, ks].T
        e_t = ea_ref[:, ks].T
        for j in range(STATE_BATCH_TILE):
            s_new = e_t[:, j:j + 1] * s_ref[j, h] + k_t[:, j:j + 1] * v_ref[j:j + 1, vs]
            so_ref[j, h] = s_new
            o_ref[j:j + 1, vs] = jnp.sum(q_t[:, j:j + 1] * s_new, axis=0, keepdims=True)


def _sample_gla_state(q, k, ea, v, state, layer, carrier):
    rows = q.shape[0]
    bt = STATE_BATCH_TILE
    state_block = (None, bt, GLA_HEADS, GLA_DK, GLA_DV)
    state_index = lambda i: (layer, i, 0, 0, 0)
    args = [q, k, ea, v, state]
    in_specs = [pl.BlockSpec((bt, HDK), lambda i: (i, 0)),
                pl.BlockSpec((bt, HDK), lambda i: (i, 0)),
                pl.BlockSpec((bt, HDK), lambda i: (i, 0)),
                pl.BlockSpec((bt, HDV), lambda i: (i, 0)),
                pl.BlockSpec(state_block, state_index)]
    aliases = {}
    if carrier is not None:
        args.append(carrier)
        in_specs.append(pl.BlockSpec(memory_space=pl.ANY))
        aliases = {len(args) - 1: 1}
    return pl.pallas_call(
        _sample_gla_state_body,
        grid=(rows // bt,),
        in_specs=in_specs,
        out_specs=[pl.BlockSpec((bt, HDV), lambda i: (i, 0)),
                   pl.BlockSpec(state_block, state_index)],
        out_shape=[jax.ShapeDtypeStruct((rows, HDV), F32),
                   jax.ShapeDtypeStruct(state.shape, F32)],
        input_output_aliases=aliases,
        name="sample_gla_state",
    )(*args)


def _sample_tail_body(x_ref, mod_ref, o_ref, ssc_ref, srg_ref, srh_ref,
                      ng_ref, wgla_ref, gng_ref, gwb_ref, wsc_ref, sccw_ref, scwb_ref, wrg_ref, rgcw_ref,
                      rgcb_ref, rgwax_ref, rgba_ref, rgbx_ref, rglam_ref, rgwb_ref, wm_ref, bm_ref, wout_ref,
                      fg_ref, y_ref, nsc_ref, nrg_ref, nrh_ref, *, final):
    d = D_MODEL
    x = x_ref[...]
    mod = mod_ref[...]
    gate = mod[:, 2 * d:3 * d]
    hb = _mod_norm(x, mod, ng_ref[...]).astype(BF16)

    zg = _mm(hb, wgla_ref[:, 2 * HDK + HDV:2 * HDK + 2 * HDV])
    parts = _gla_out_norm_gate(o_ref[...], zg, gng_ref[...])
    y_gla = _mm(jnp.concatenate(parts, axis=1), gwb_ref[...])

    u = _mm(hb, wsc_ref[:, d:2 * d]) * _mm(hb, wsc_ref[:, 2 * d:3 * d])
    cw = sccw_ref[...]
    s0, s1 = ssc_ref[:, 0:d], ssc_ref[:, d:2 * d]
    uc = s0 * cw[0:1] + s1 * cw[1:2] + u * cw[2:3]
    nsc_ref[:, 0:d] = s1
    nsc_ref[:, d:2 * d] = u
    zb = _mm(hb, wsc_ref[:, 0:d])
    zgs = _mm(hb, wsc_ref[:, 3 * d:4 * d])
    y_sc = _mm(zb * uc * _silu(zgs), scwb_ref[...])

    zx = _mm(hb, wrg_ref[:, 0:d])
    rw = rgcw_ref[...]
    r0, r1, r2 = srg_ref[:, 0:d], srg_ref[:, d:2 * d], srg_ref[:, 2 * d:3 * d]
    xc = r0 * rw[0:1] + r1 * rw[1:2] + r2 * rw[2:3] + zx * rw[3:4] + rgcb_ref[...]
    nrg_ref[:, 0:d] = r1
    nrg_ref[:, d:2 * d] = r2
    nrg_ref[:, 2 * d:3 * d] = zx
    h_parts = []
    for cb in range(d // COL_BLOCK):
        cs = _col(cb)
        r_pre, i_pre = _rg_gate_logits(xc[:, cs], cb, rgwax_ref)
        a, bt = _rg_coeffs(xc[:, cs], r_pre, i_pre, cb, rgba_ref, rgbx_ref, rglam_ref)
        h_parts.append(a * srh_ref[:, cs] + bt)
    hr = jnp.concatenate(h_parts, axis=1)
    nrh_ref[...] = hr
    zgr = _mm(hb, wrg_ref[:, d:2 * d])
    y_rg = _mm(hr * _silu(zgr), rgwb_ref[...])

    logits = tuple(_mm(hb, wm_ref[:, n * d:(n + 1) * d]) for n in range(3))
    xn = _merge_out(x, gate, (y_gla, y_sc, y_rg), logits, bm_ref, wout_ref)
    if final:
        xn = _rms(xn, fg_ref[...])
    y_ref[...] = xn


def _sample_tail(x, mod_all, o, ssc, srg, srh, p, final_gain, layer, final):
    rows, d = x.shape
    weights = tuple(p[name] for name in ("ng", "wgla", "gng", "gwb", "wsc", "sccw", "scwb", "wrg", "rgcw",
                                         "rgcb", "rgwax", "rgba", "rgbx", "rglam", "rgwb", "wm", "bm", "wout"))
    out_shape = [jax.ShapeDtypeStruct((rows, d), F32),
                 jax.ShapeDtypeStruct((rows, (SC_CONV_W - 1) * d), F32),
                 jax.ShapeDtypeStruct((rows, (RG_CONV_W - 1) * d), F32),
                 jax.ShapeDtypeStruct((rows, d), F32)]
    in_specs = [_whole_spec(x), _sample_mod_spec(rows, layer), _whole_spec(o),
                _layer_spec(ssc, layer), _layer_spec(srg, layer), _layer_spec(srh, layer)]
    in_specs += [_layer_spec(w, layer) for w in weights] + [_whole_spec(final_gain)]
    return pl.pallas_call(
        functools.partial(_sample_tail_body, final=final),
        grid=(1,),
        in_specs=in_specs,
        out_specs=[pl.BlockSpec(s.shape, lambda i: (0, 0)) for s in out_shape],
        out_shape=out_shape,
        compiler_params=pltpu.CompilerParams(vmem_limit_bytes=VMEM_LIMIT),
        name="sample_tail",
    )(x, mod_all, o, ssc, srg, srh, *weights, final_gain)


def _stacked_params(norm_gain, w_in, gla_w_alpha, gla_b_alpha, gla_norm_gain, gla_w_branch, sc_conv_w,
                    sc_w_branch, rg_conv_w, rg_conv_b, rg_w_a, rg_b_a, rg_w_x, rg_b_x, rg_lambda,
                    rg_w_branch, b_merge, w_out):
    d = D_MODEL
    return {
        **_split_w_in(w_in),
        "ng": norm_gain.reshape(DEPTH, 1, d),
        "gwa": jnp.pad(gla_w_alpha.astype(BF16), ((0, 0), (0, LANES - GLA_RANK), (0, 0))),
        "gba": gla_b_alpha.reshape(DEPTH, 1, HDK),
        "gng": gla_norm_gain.reshape(DEPTH, 1, GLA_DV),
        "gwb": gla_w_branch.astype(BF16),
        "sccw": sc_conv_w,
        "scwb": sc_w_branch.astype(BF16),
        "rgcw": rg_conv_w,
        "rgcb": rg_conv_b.reshape(DEPTH, 1, d),
        "rgwax": jnp.concatenate([rg_w_a, rg_w_x], axis=-1).astype(BF16),
        "rgba": rg_b_a.reshape(DEPTH, 1, d),
        "rgbx": rg_b_x.reshape(DEPTH, 1, d),
        "rglam": rg_lambda.reshape(DEPTH, 1, d),
        "rgwb": rg_w_branch.astype(BF16),
        "bm": b_merge.reshape(DEPTH, 1, 3 * d),
        "wout": w_out.astype(BF16),
    }


def kernel(x_prompt, x_sample, c_prompt, c_sample, state_gla, state_sc_conv, state_rg_conv, state_rg_h, w_ada, b_ada, norm_gain, w_in, gla_w_alpha, gla_b_alpha, gla_norm_gain, gla_w_branch, sc_conv_w, sc_w_branch, rg_conv_w, rg_conv_b, rg_w_a, rg_b_a, rg_w_x, rg_b_x, rg_lambda, rg_w_branch, b_merge, w_out, final_gain):
    assert DEPTH == 2
    d = D_MODEL
    bp = x_prompt.shape[0]
    bs = x_sample.shape[0]
    fg = final_gain.reshape(1, d)
    mod_all = _ada_mod(jnp.concatenate([c_sample, c_prompt], axis=0), w_ada, b_ada)
    p = _stacked_params(norm_gain, w_in, gla_w_alpha, gla_b_alpha, gla_norm_gain, gla_w_branch,
                        sc_conv_w, sc_w_branch, rg_conv_w, rg_conv_b, rg_w_a, rg_b_a, rg_w_x, rg_b_x,
                        rg_lambda, rg_w_branch, b_merge, w_out)
    ssc = state_sc_conv.reshape(DEPTH, bs, (SC_CONV_W - 1) * d)
    srg = state_rg_conv.reshape(DEPTH, bs, (RG_CONV_W - 1) * d)

    xp = x_prompt
    xs = x_sample.reshape(bs, d)
    gla_p, sc_p, rgc_p, rgh_p = [], [], [], []
    gla_s, sc_s, rgc_s, rgh_s = None, [], [], []
    for l in range(DEPTH):
        final = l == DEPTH - 1
        xp, g, sc, rc, rh = _prompt_layer(xp, mod_all, bs, p, fg, l, final)
        gla_p.append(g)
        sc_p.append(sc)
        rgc_p.append(rc)
        rgh_p.append(rh.reshape(bp, d))

        q, k, v, ea = _sample_gla_prep(xs, mod_all, p, l)
        o, gla_s = _sample_gla_state(q, k, ea, v, state_gla, l, gla_s)
        xs, nsc, nrg, nrh = _sample_tail(xs, mod_all, o, ssc, srg, state_rg_h, p, fg, l, final)
        sc_s.append(nsc.reshape(bs, SC_CONV_W - 1, d))
        rgc_s.append(nrg.reshape(bs, RG_CONV_W - 1, d))
        rgh_s.append(nrh)

    return (xp, xs.reshape(bs, 1, d),
            jnp.stack(gla_p), jnp.stack(sc_p), jnp.stack(rgc_p), jnp.stack(rgh_p),
            gla_s, jnp.stack(sc_s), jnp.stack(rgc_s), jnp.stack(rgh_s))
```

```python
import functools

import jax
import jax.numpy as jnp
from jax import lax
from jax.experimental import pallas as pl
from jax.experimental.pallas import tpu as pltpu

F32 = jnp.float32
BF16 = jnp.bfloat16

D_MODEL = 1024
DEPTH = 2
GLA_HEADS = 4
GLA_DK = 128
GLA_DV = 256
GLA_RANK = 16
GLA_GATE_NORM = 16.0
HDK = GLA_HEADS * GLA_DK
HDV = GLA_HEADS * GLA_DV
SC_CONV_W = 3
RG_CONV_W = 4
RG_BLOCKS = 8
RG_BLOCK = 128
RG_C = 8.0
RMS_EPS = 1e-6

LANES = 128
SUBLANES = 8
ROW_TILE = 256
GLA_CHUNK = 64
GLA_HALF = GLA_CHUNK // 2
COL_BLOCK = 256
SEG_STEPS = ROW_TILE // SUBLANES
SLAB_PITCH = SEG_STEPS + 4
STATE_BATCH_TILE = 8
VMEM_LIMIT = 60 * 1024 * 1024

OFF_GLA = 0
OFF_LR = OFF_GLA + 2 * HDK + 2 * HDV
OFF_SC = OFF_LR + GLA_RANK
OFF_RG = OFF_SC + 4 * D_MODEL
OFF_MERGE = OFF_RG + 2 * D_MODEL
N_IN = OFF_MERGE + 3 * D_MODEL


def _mm(a, w):
    return jnp.dot(a.astype(BF16), w, preferred_element_type=F32)


def _mm_nt(a, b):
    return lax.dot_general(a.astype(BF16), b.astype(BF16), (((1,), (1,)), ((), ())),
                           preferred_element_type=F32)


def _mm_tn(a, b):
    return lax.dot_general(a.astype(BF16), b.astype(BF16), (((0,), (0,)), ((), ())),
                           preferred_element_type=F32)


def _rms(x, g):
    ms = jnp.mean(x * x, axis=-1, keepdims=True)
    return x * lax.rsqrt(ms + RMS_EPS) * g


def _silu(x):
    return x * jax.nn.sigmoid(x)


def _mod_norm(x, mod, g):
    d = D_MODEL
    shift, scale = mod[:, 0:d], mod[:, d:2 * d]
    return _rms(x, g) * (1.0 + scale) + shift


def _gla_log_decay(hb, wlr_ref, gwa_ref, gba_ref):
    zlr = _mm(hb, wlr_ref[...])
    pre = _mm(zlr, gwa_ref[...]) + gba_ref[...]
    return jax.nn.log_sigmoid(pre) * (1.0 / GLA_GATE_NORM)


def _gla_out_norm_gate(o, zg, gng):
    parts = []
    for h in range(GLA_HEADS):
        sl = slice(h * GLA_DV, (h + 1) * GLA_DV)
        parts.append(_rms(o[:, sl], gng) * _silu(zg[:, sl]))
    return parts


def _col(cb, base=0):
    return slice(base + cb * COL_BLOCK, base + (cb + 1) * COL_BLOCK)


def _acc(total, part):
    return part if total is None else total + part


def _rg_gate_logits(xc, cb, rgwax_ref):
    xb = xc.astype(BF16)
    per_cb = COL_BLOCK // RG_BLOCK
    r_parts, i_parts = [], []
    for n in range(per_cb):
        ri = jnp.dot(xb[:, n * RG_BLOCK:(n + 1) * RG_BLOCK], rgwax_ref[cb * per_cb + n],
                     preferred_element_type=F32)
        r_parts.append(ri[:, 0:RG_BLOCK])
        i_parts.append(ri[:, RG_BLOCK:2 * RG_BLOCK])
    return jnp.concatenate(r_parts, axis=1), jnp.concatenate(i_parts, axis=1)


def _rg_coeffs(xc, r_pre, i_pre, cb, ba_ref, bx_ref, lam_ref):
    cs = _col(cb)
    r = jax.nn.sigmoid(r_pre + ba_ref[:, cs])
    ig = jax.nn.sigmoid(i_pre + bx_ref[:, cs])
    lam = lam_ref[:, cs]
    softplus_neg_lam = jnp.log1p(jnp.exp(-jnp.abs(lam))) + jnp.maximum(-lam, 0.0)
    log_a = (-RG_C) * r * softplus_neg_lam
    a = jnp.exp(log_a)
    rad = 1.0 - a * a
    root = jnp.where(rad > 0.0, rad * lax.rsqrt(rad), 0.0)
    bterm = root * (ig * xc)
    return a, bterm


def _merge_out(x, gate, ys, logits, bm_ref, wout_ref):
    d = D_MODEL
    mix = None
    for n, (y, m) in enumerate(zip(ys, logits)):
        mix = _acc(mix, jax.nn.sigmoid(m + bm_ref[:, n * d:(n + 1) * d]) * y)
    return x + gate * _mm(mix, wout_ref[...])


def _ada_body(c_ref, w_ref, b_ref, o_ref):
    o_ref[...] = _mm(_silu(c_ref[...]), w_ref[...].astype(BF16)) + b_ref[...]


def _ada_mod(c_all, w_ada, b_ada):
    rows = c_all.shape[0]
    d = D_MODEL
    return pl.pallas_call(
        _ada_body,
        grid=(DEPTH, 3),
        in_specs=[pl.BlockSpec((rows, d), lambda l, j: (0, 0)),
                  pl.BlockSpec((None, d, d), lambda l, j: (l, 0, j)),
                  pl.BlockSpec((None, 1, d), lambda l, j: (l, 0, j))],
        out_specs=pl.BlockSpec((None, rows, d), lambda l, j: (l, 0, j)),
        out_shape=jax.ShapeDtypeStruct((DEPTH, rows, 3 * d), F32),
        name="ada_mod",
    )(c_all, w_ada, b_ada.reshape(DEPTH, 1, 3 * d))


W_IN_PIECES = (("wgla", OFF_GLA, OFF_LR), ("wsc", OFF_SC, OFF_RG), ("wrg", OFF_RG, OFF_MERGE),
               ("wm", OFF_MERGE, N_IN))
W_IN_SPLIT_BLOCK = 1024


def _split_piece_body(wt_ref, o_ref, *, valid):
    w = wt_ref[0].T
    if valid < w.shape[1]:
        lane = lax.broadcasted_iota(jnp.int32, w.shape, 1)
        w = jnp.where(lane < valid, w, 0.0)
    o_ref[...] = w.astype(BF16)


def _split_piece(w_in_t, lo, width, block, name):
    d = D_MODEL
    n_blocks = -(-width // block)
    in_block = (pl.Element(1), pl.Element(block), pl.Element(d))
    return pl.pallas_call(
        functools.partial(_split_piece_body, valid=min(width, block)),
        grid=(DEPTH, n_blocks),
        in_specs=[pl.BlockSpec(in_block, lambda l, i: (l, pl.multiple_of(lo + block * i, 16), 0))],
        out_specs=pl.BlockSpec((None, d, block), lambda l, i: (l, 0, i)),
        out_shape=jax.ShapeDtypeStruct((DEPTH, d, n_blocks * block), BF16),
        compiler_params=pltpu.CompilerParams(vmem_limit_bytes=VMEM_LIMIT),
        name=name,
    )(w_in_t)


def _split_w_in(w_in):
    w_in_t = jnp.swapaxes(w_in, 1, 2)
    pieces = {name: _split_piece(w_in_t, lo, hi - lo, W_IN_SPLIT_BLOCK, "split_" + name)
              for name, lo, hi in W_IN_PIECES}
    pieces["wlr"] = _split_piece(w_in_t, OFF_LR, GLA_RANK, LANES, "split_wlr")
    return pieces


def _prompt_layer_body(x_ref, mod_ref, ng_ref, wgla_ref, wlr_ref, gwa_ref, gba_ref, gng_ref, gwb_ref,
                       wsc_ref, sccw_ref, scwb_ref, wrg_ref, rgcw_ref, rgcb_ref, rgwax_ref, rgba_ref,
                       rgbx_ref, rglam_ref, rgwb_ref, wm_ref, bm_ref, wout_ref, fg_ref,
                       y_ref, gla_ref, sct_ref, rgt_ref, rgh_ref,
                       st_ref, hcar_ref, slab_h_ref, slab_g_ref, slab_o_ref, *, final):
    d = D_MODEL
    i = pl.program_id(1)

    @pl.when(i == 0)
    def _():
        st_ref[...] = jnp.zeros(st_ref.shape, F32)
        sct_ref[...] = jnp.zeros(sct_ref.shape, F32)
        rgt_ref[...] = jnp.zeros(rgt_ref.shape, F32)
        hcar_ref[...] = jnp.zeros(hcar_ref.shape, F32)

    x = x_ref[...]
    mod = mod_ref[...]
    gate = mod[:, 2 * d:3 * d]
    h = _mod_norm(x, mod, ng_ref[...])
    hb = h.astype(BF16)
    hb_s = _to_segment_major(h, slab_h_ref).astype(BF16)

    y_rg = _from_segment_major(
        _prompt_rg_branch(hb_s, wrg_ref, rgcw_ref, rgcb_ref, rgwax_ref, rgba_ref, rgbx_ref, rglam_ref,
                          rgwb_ref, hcar_ref, rgt_ref, rgh_ref), slab_o_ref)
    t_gla = _prompt_gla_branch(hb, wgla_ref, wlr_ref, gwa_ref, gba_ref, gng_ref, st_ref)
    y_gla = None
    for hd in range(GLA_HEADS):
        y_gla = _acc(y_gla, _mm(t_gla[hd], gwb_ref[hd * GLA_DV:(hd + 1) * GLA_DV, :]))
    y_sc = _from_segment_major(
        _mm(_prompt_sc_branch(hb_s, wsc_ref, sccw_ref, sct_ref), scwb_ref[...]), slab_g_ref)
    logits = tuple(_mm(hb, wm_ref[:, n * d:(n + 1) * d]) for n in range(3))
    xn = _merge_out(x, gate, (y_gla, y_sc, y_rg), logits, bm_ref, wout_ref)
    if final:
        xn = _rms(xn, fg_ref[...])
    y_ref[...] = xn

    @pl.when(i == pl.num_programs(1) - 1)
    def _():
        for hd in range(GLA_HEADS):
            gla_ref[hd] = st_ref[hd].T


def _to_segment_major(v, slab_ref):
    for lb in range(D_MODEL // LANES):
        for j in range(SUBLANES):
            slab_ref[lb, j * SLAB_PITCH:j * SLAB_PITCH + SEG_STEPS, :] = (
                v[j * SEG_STEPS:(j + 1) * SEG_STEPS, lb * LANES:(lb + 1) * LANES])
    steps = []
    for s in range(SEG_STEPS):
        steps.append(jnp.concatenate(
            [slab_ref[lb, pl.ds(s, SUBLANES, stride=SLAB_PITCH), :] for lb in range(D_MODEL // LANES)], axis=1))
    return jnp.concatenate(steps, axis=0)


def _from_segment_major(v, slab_ref):
    for s in range(SEG_STEPS):
        for lb in range(D_MODEL // LANES):
            slab_ref[lb, pl.ds(s, SUBLANES, stride=SLAB_PITCH), :] = (
                v[s * SUBLANES:(s + 1) * SUBLANES, lb * LANES:(lb + 1) * LANES])
    cols = []
    for lb in range(D_MODEL // LANES):
        cols.append(jnp.concatenate(
            [slab_ref[lb, j * SLAB_PITCH:j * SLAB_PITCH + SEG_STEPS, :] for j in range(SUBLANES)], axis=0))
    return jnp.concatenate(cols, axis=1)


def _seg_conv(v, w, tails, bias=None):
    k = w.shape[0] - 1
    width = v.shape[1]
    sub = lax.broadcasted_iota(jnp.int32, (SUBLANES, width), 0)
    out = v * w[k:k + 1]
    if bias is not None:
        out = out + bias
    for back in range(1, k + 1):
        heads = []
        for r in range(back):
            s = SEG_STEPS - back + r
            wrapped = pltpu.roll(v[s * SUBLANES:(s + 1) * SUBLANES, :], 1, 0)
            heads.append(jnp.where(sub == 0, tails[k - back + r:k - back + r + 1, :], wrapped))
        shifted = jnp.concatenate(heads + [v[0:(SEG_STEPS - back) * SUBLANES, :]], axis=0)
        out = out + shifted * w[k - back:k - back + 1]
    return out


def _seg_tails(v, k):
    return jnp.concatenate(
        [v[(SEG_STEPS - q) * SUBLANES + SUBLANES - 1:(SEG_STEPS - q + 1) * SUBLANES, :] for q in range(k, 0, -1)],
        axis=0)


def _seg_scan(a, b, h0):
    def step(s, v):
        return v[s * SUBLANES:(s + 1) * SUBLANES, :]
    prod, loc = step(0, a), step(0, b)
    for s in range(1, SEG_STEPS):
        loc = step(s, a) * loc + step(s, b)
        prod = prod * step(s, a)
    h_in = h0
    entering = []
    for j in range(SUBLANES):
        entering.append(h_in)
        h_in = prod[j:j + 1] * h_in + loc[j:j + 1]
    cur = jnp.concatenate(entering, axis=0)
    outs = []
    for s in range(SEG_STEPS):
        cur = step(s, a) * cur + step(s, b)
        outs.append(cur)
    return jnp.concatenate(outs, axis=0), h_in


def _prompt_gla_branch(hb, wgla_ref, wlr_ref, gwa_ref, gba_ref, gng_ref, st_ref):
    tt = ROW_TILE
    zlr = _mm(hb, wlr_ref[...])
    zq = _mm(hb, wgla_ref[:, 0:HDK]) * (GLA_DK ** -0.5)
    pre = _mm(zlr, gwa_ref[...]) + gba_ref[...]
    zk = _mm(hb, wgla_ref[:, HDK:2 * HDK])
    la = jax.nn.log_sigmoid(pre) * (1.0 / GLA_GATE_NORM)
    row = lax.broadcasted_iota(jnp.int32, (tt, tt), 0)
    col = lax.broadcasted_iota(jnp.int32, (tt, tt), 1)
    chunk_bits = GLA_CHUNK.bit_length() - 1
    causal = jnp.logical_and(jnp.right_shift(row, chunk_bits) == jnp.right_shift(col, chunk_bits), col <= row)
    tri = jnp.where(causal, 1.0, 0.0).astype(BF16)
    la_hi = la.astype(BF16)
    la_lo = (la - la_hi.astype(F32)).astype(BF16)
    bcum = (jnp.dot(tri, la_hi, preferred_element_type=F32)
            + jnp.dot(tri, la_lo, preferred_element_type=F32))
    zv = _mm(hb, wgla_ref[:, 2 * HDK:2 * HDK + HDV])
    zg = _mm(hb, wgla_ref[:, 2 * HDK + HDV:2 * HDK + 2 * HDV])
    gng = gng_ref[...]
    n_chunk = tt // GLA_CHUNK
    gated = []
    for h in range(GLA_HEADS):
        ks = slice(h * GLA_DK, (h + 1) * GLA_DK)
        vs = slice(h * GLA_DV, (h + 1) * GLA_DV)
        b = bcum[:, ks]
        b3 = b.reshape(n_chunk, GLA_CHUNK, GLA_DK)
        b_mid = jnp.broadcast_to(b3[:, GLA_HALF - 1:GLA_HALF, :], b3.shape).reshape(tt, GLA_DK)
        b_end = jnp.broadcast_to(b3[:, GLA_CHUNK - 1:GLA_CHUNK, :], b3.shape).reshape(tt, GLA_DK)
        q = zq[:, ks]
        k = zk[:, ks]
        qe = q * jnp.exp(b - b_mid)
        ke = k * jnp.exp(b_mid - b)
        qs = (q * jnp.exp(b)).astype(BF16)
        kd = (k * jnp.exp(b_end - b)).astype(BF16)
        att = jnp.where(causal, _mm_nt(qe, ke), 0.0)
        vb = zv[:, vs].astype(BF16)
        o = jnp.dot(att.astype(BF16), vb, preferred_element_type=F32)
        st = st_ref[h]
        o_parts = []
        for c in range(n_chunk):
            rs = slice(c * GLA_CHUNK, (c + 1) * GLA_CHUNK)
            o_parts.append(o[rs] + _mm_nt(qs[rs], st))
            e_end = jnp.exp(b3[c, GLA_CHUNK - 1:GLA_CHUNK, :])
            st = st * e_end + _mm_tn(vb[rs], kd[rs])
        st_ref[h] = st
        gated.append(_rms(jnp.concatenate(o_parts, axis=0), gng) * _silu(zg[:, vs]))
    return gated


def _prompt_sc_branch(hb, wsc_ref, sccw_ref, sct_ref):
    d = D_MODEL
    u = _mm(hb, wsc_ref[:, d:2 * d]) * _mm(hb, wsc_ref[:, 2 * d:3 * d])
    uc = _seg_conv(u, sccw_ref[...], sct_ref[...])
    sct_ref[...] = _seg_tails(u, SC_CONV_W - 1)
    zb = _mm(hb, wsc_ref[:, 0:d])
    zgs = _mm(hb, wsc_ref[:, 3 * d:4 * d])
    return zb * uc * _silu(zgs)


def _prompt_rg_branch(hb, wrg_ref, rgcw_ref, rgcb_ref, rgwax_ref, rgba_ref, rgbx_ref, rglam_ref, rgwb_ref,
                      hcar_ref, rgt_ref, rgh_ref):
    d = D_MODEL
    n_cb = d // COL_BLOCK
    rw = rgcw_ref[...]
    old_tails = rgt_ref[...]

    def project(cb):
        return _mm(hb, wrg_ref[:, _col(cb)]), _mm(hb, wrg_ref[:, _col(cb, d)])

    def conv_and_gate_logits(cb, zx):
        cs = _col(cb)
        xc = _seg_conv(zx, rw[:, cs], old_tails[:, cs], rgcb_ref[:, cs])
        rgt_ref[:, cs] = _seg_tails(zx, RG_CONV_W - 1)
        return (xc,) + _rg_gate_logits(xc, cb, rgwax_ref)

    def recur(cb, xc, r_pre, i_pre, zgr):
        cs = _col(cb)
        a, bt = _rg_coeffs(xc, r_pre, i_pre, cb, rgba_ref, rgbx_ref, rglam_ref)
        h_all, h_last = _seg_scan(a, bt, hcar_ref[:, cs])
        hcar_ref[:, cs] = h_last
        rgh_ref[:, cs] = h_last
        return h_all * _silu(zgr)

    proj = {0: project(0), 1: project(1)}
    logits = {0: conv_and_gate_logits(0, proj[0][0])}
    y = None
    for cb in range(n_cb):
        if cb + 2 < n_cb:
            proj[cb + 2] = project(cb + 2)
        if cb + 1 < n_cb:
            logits[cb + 1] = conv_and_gate_logits(cb + 1, proj[cb + 1][0])
        t = recur(cb, *logits[cb], proj[cb][1])
        y = _acc(y, _mm(t, rgwb_ref[_col(cb), :]))
    return y


def _layer_spec(stacked, layer):
    rest = stacked.shape[1:]
    return pl.BlockSpec((None,) + rest, lambda *_: (layer,) + (0,) * len(rest), pipeline_mode=pl.Buffered(1))


def _whole_spec(arr):
    return pl.BlockSpec(arr.shape, lambda *_: (0,) * arr.ndim, pipeline_mode=pl.Buffered(1))


PROMPT_WEIGHTS = ("ng", "wgla", "wlr", "gwa", "gba", "gng", "gwb", "wsc", "sccw", "scwb", "wrg", "rgcw",
                  "rgcb", "rgwax", "rgba", "rgbx", "rglam", "rgwb", "wm", "bm", "wout")


def _prompt_layer(x, mod_all, mod_row0, p, final_gain, layer, final):
    bsz, seq, d = x.shape
    tt = ROW_TILE
    weights = tuple(p[name] for name in PROMPT_WEIGHTS)
    mod4 = mod_all.reshape(DEPTH, mod_all.shape[1], 1, 3 * d)
    in_specs = [pl.BlockSpec((None, tt, d), lambda b, i: (b, i, 0)),
                pl.BlockSpec((None, None, 1, 3 * d), lambda b, i: (layer, mod_row0 + b, 0, 0))]
    in_specs += [_layer_spec(w, layer) for w in weights] + [_whole_spec(final_gain)]
    out_specs = [pl.BlockSpec((None, tt, d), lambda b, i: (b, i, 0)),
                 pl.BlockSpec((None, GLA_HEADS, GLA_DK, GLA_DV), lambda b, i: (b, 0, 0, 0)),
                 pl.BlockSpec((None, SC_CONV_W - 1, d), lambda b, i: (b, 0, 0)),
                 pl.BlockSpec((None, RG_CONV_W - 1, d), lambda b, i: (b, 0, 0)),
                 pl.BlockSpec((None, 1, d), lambda b, i: (b, 0, 0))]
    out_shape = [jax.ShapeDtypeStruct((bsz, seq, d), F32),
                 jax.ShapeDtypeStruct((bsz, GLA_HEADS, GLA_DK, GLA_DV), F32),
                 jax.ShapeDtypeStruct((bsz, SC_CONV_W - 1, d), F32),
                 jax.ShapeDtypeStruct((bsz, RG_CONV_W - 1, d), F32),
                 jax.ShapeDtypeStruct((bsz, 1, d), F32)]
    scratch = [pltpu.VMEM((GLA_HEADS, GLA_DV, GLA_DK), F32),
               pltpu.VMEM((1, d), F32)]
    scratch += [pltpu.VMEM((d // LANES, SUBLANES * SLAB_PITCH, LANES), F32)] * 3
    return pl.pallas_call(
        functools.partial(_prompt_layer_body, final=final),
        grid=(bsz, seq // tt),
        in_specs=in_specs,
        out_specs=out_specs,
        out_shape=out_shape,
        scratch_shapes=scratch,
        compiler_params=pltpu.CompilerParams(
            dimension_semantics=("arbitrary", "arbitrary"), vmem_limit_bytes=VMEM_LIMIT),
        name="prompt_layer",
    )(x, mod4, *weights, final_gain)


def _sample_gla_prep_body(x_ref, mod_ref, ng_ref, wgla_ref, wlr_ref, gwa_ref, gba_ref,
                          q_ref, k_ref, v_ref, ea_ref):
    hb = _mod_norm(x_ref[...], mod_ref[...], ng_ref[...]).astype(BF16)
    la = _gla_log_decay(hb, wlr_ref, gwa_ref, gba_ref)
    ea_ref[...] = jnp.exp(la)
    q_ref[...] = _mm(hb, wgla_ref[:, 0:HDK]) * (GLA_DK ** -0.5)
    k_ref[...] = _mm(hb, wgla_ref[:, HDK:2 * HDK])
    v_ref[...] = _mm(hb, wgla_ref[:, 2 * HDK:2 * HDK + HDV])


def _sample_mod_spec(rows, layer):
    return pl.BlockSpec((None, rows, 3 * D_MODEL), lambda *_: (layer, 0, 0))


def _sample_gla_prep(x, mod_all, p, layer):
    rows = x.shape[0]
    weights = tuple(p[name] for name in ("ng", "wgla", "wlr", "gwa", "gba"))
    out_shape = [jax.ShapeDtypeStruct((rows, HDK), F32),
                 jax.ShapeDtypeStruct((rows, HDK), F32),
                 jax.ShapeDtypeStruct((rows, HDV), F32),
                 jax.ShapeDtypeStruct((rows, HDK), F32)]
    return pl.pallas_call(
        _sample_gla_prep_body,
        grid=(1,),
        in_specs=[_whole_spec(x), _sample_mod_spec(rows, layer)] + [_layer_spec(w, layer) for w in weights],
        out_specs=[pl.BlockSpec(s.shape, lambda i: (0, 0)) for s in out_shape],
        out_shape=out_shape,
        compiler_params=pltpu.CompilerParams(vmem_limit_bytes=VMEM_LIMIT),
        name="sample_gla_prep",
    )(x, mod_all, *weights)


def _sample_gla_state_body(q_ref, k_ref, ea_ref, v_ref, s_ref, *rest):
    o_ref, so_ref = rest[-2:]
    for h in range(GLA_HEADS):
        ks = slice(h * GLA_DK, (h + 1) * GLA_DK)
        vs = slice(h * GLA_DV, (h + 1) * GLA_DV)
        q_t = q_ref[:, ks].T
        k_t = k_ref[:, ks].T
        e_t = ea_ref[:, ks].T
        for j in range(STATE_BATCH_TILE):
            s_new = e_t[:, j:j + 1] * s_ref[j, h] + k_t[:, j:j + 1] * v_ref[j:j + 1, vs]
            so_ref[j, h] = s_new
            o_ref[j:j + 1, vs] = jnp.sum(q_t[:, j:j + 1] * s_new, axis=0, keepdims=True)


def _sample_gla_state(q, k, ea, v, state, layer, carrier):
    rows = q.shape[0]
    bt = STATE_BATCH_TILE
    state_block = (None, bt, GLA_HEADS, GLA_DK, GLA_DV)
    state_index = lambda i: (layer, i, 0, 0, 0)
    args = [q, k, ea, v, state]
    in_specs = [pl.BlockSpec((bt, HDK), lambda i: (i, 0)),
                pl.BlockSpec((bt, HDK), lambda i: (i, 0)),
                pl.BlockSpec((bt, HDK), lambda i: (i, 0)),
                pl.BlockSpec((bt, HDV), lambda i: (i, 0)),
                pl.BlockSpec(state_block, state_index)]
    aliases = {}
    if carrier is not None:
        args.append(carrier)
        in_specs.append(pl.BlockSpec(memory_space=pl.ANY))
        aliases = {len(args) - 1: 1}
    return pl.pallas_call(
        _sample_gla_state_body,
        grid=(rows // bt,),
        in_specs=in_specs,
        out_specs=[pl.BlockSpec((bt, HDV), lambda i: (i, 0)),
                   pl.BlockSpec(state_block, state_index)],
        out_shape=[jax.ShapeDtypeStruct((rows, HDV), F32),
                   jax.ShapeDtypeStruct(state.shape, F32)],
        input_output_aliases=aliases,
        name="sample_gla_state",
    )(*args)


def _sample_tail_body(x_ref, mod_ref, o_ref, ssc_ref, srg_ref, srh_ref,
                      ng_ref, wgla_ref, gng_ref, gwb_ref, wsc_ref, sccw_ref, scwb_ref, wrg_ref, rgcw_ref,
                      rgcb_ref, rgwax_ref, rgba_ref, rgbx_ref, rglam_ref, rgwb_ref, wm_ref, bm_ref, wout_ref,
                      fg_ref, y_ref, nsc_ref, nrg_ref, nrh_ref, *, final):
    d = D_MODEL
    x = x_ref[...]
    mod = mod_ref[...]
    gate = mod[:, 2 * d:3 * d]
    hb = _mod_norm(x, mod, ng_ref[...]).astype(BF16)

    zg = _mm(hb, wgla_ref[:, 2 * HDK + HDV:2 * HDK + 2 * HDV])
    parts = _gla_out_norm_gate(o_ref[...], zg, gng_ref[...])
    y_gla = _mm(jnp.concatenate(parts, axis=1), gwb_ref[...])

    u = _mm(hb, wsc_ref[:, d:2 * d]) * _mm(hb, wsc_ref[:, 2 * d:3 * d])
    cw = sccw_ref[...]
    s0, s1 = ssc_ref[:, 0:d], ssc_ref[:, d:2 * d]
    uc = s0 * cw[0:1] + s1 * cw[1:2] + u * cw[2:3]
    nsc_ref[:, 0:d] = s1
    nsc_ref[:, d:2 * d] = u
    zb = _mm(hb, wsc_ref[:, 0:d])
    zgs = _mm(hb, wsc_ref[:, 3 * d:4 * d])
    y_sc = _mm(zb * uc * _silu(zgs), scwb_ref[...])

    zx = _mm(hb, wrg_ref[:, 0:d])
    rw = rgcw_ref[...]
    r0, r1, r2 = srg_ref[:, 0:d], srg_ref[:, d:2 * d], srg_ref[:, 2 * d:3 * d]
    xc = r0 * rw[0:1] + r1 * rw[1:2] + r2 * rw[2:3] + zx * rw[3:4] + rgcb_ref[...]
    nrg_ref[:, 0:d] = r1
    nrg_ref[:, d:2 * d] = r2
    nrg_ref[:, 2 * d:3 * d] = zx
    h_parts = []
    for cb in range(d // COL_BLOCK):
        cs = _col(cb)
        r_pre, i_pre = _rg_gate_logits(xc[:, cs], cb, rgwax_ref)
        a, bt = _rg_coeffs(xc[:, cs], r_pre, i_pre, cb, rgba_ref, rgbx_ref, rglam_ref)
        h_parts.append(a * srh_ref[:, cs] + bt)
    hr = jnp.concatenate(h_parts, axis=1)
    nrh_ref[...] = hr
    zgr = _mm(hb, wrg_ref[:, d:2 * d])
    y_rg = _mm(hr * _silu(zgr), rgwb_ref[...])

    logits = tuple(_mm(hb, wm_ref[:, n * d:(n + 1) * d]) for n in range(3))
    xn = _merge_out(x, gate, (y_gla, y_sc, y_rg), logits, bm_ref, wout_ref)
    if final:
        xn = _rms(xn, fg_ref[...])
    y_ref[...] = xn


def _sample_tail(x, mod_all, o, ssc, srg, srh, p, final_gain, layer, final):
    rows, d = x.shape
    weights = tuple(p[name] for name in ("ng", "wgla", "gng", "gwb", "wsc", "sccw", "scwb", "wrg", "rgcw",
                                         "rgcb", "rgwax", "rgba", "rgbx", "rglam", "rgwb", "wm", "bm", "wout"))
    out_shape = [jax.ShapeDtypeStruct((rows, d), F32),
                 jax.ShapeDtypeStruct((rows, (SC_CONV_W - 1) * d), F32),
                 jax.ShapeDtypeStruct((rows, (RG_CONV_W - 1) * d), F32),
                 jax.ShapeDtypeStruct((rows, d), F32)]
    in_specs = [_whole_spec(x), _sample_mod_spec(rows, layer), _whole_spec(o),
                _layer_spec(ssc, layer), _layer_spec(srg, layer), _layer_spec(srh, layer)]
    in_specs += [_layer_spec(w, layer) for w in weights] + [_whole_spec(final_gain)]
    return pl.pallas_call(
        functools.partial(_sample_tail_body, final=final),
        grid=(1,),
        in_specs=in_specs,
        out_specs=[pl.BlockSpec(s.shape, lambda i: (0, 0)) for s in out_shape],
        out_shape=out_shape,
        compiler_params=pltpu.CompilerParams(vmem_limit_bytes=VMEM_LIMIT),
        name="sample_tail",
    )(x, mod_all, o, ssc, srg, srh, *weights, final_gain)


def _stacked_params(norm_gain, w_in, gla_w_alpha, gla_b_alpha, gla_norm_gain, gla_w_branch, sc_conv_w,
                    sc_w_branch, rg_conv_w, rg_conv_b, rg_w_a, rg_b_a, rg_w_x, rg_b_x, rg_lambda,
                    rg_w_branch, b_merge, w_out):
    d = D_MODEL
    return {
        **_split_w_in(w_in),
        "ng": norm_gain.reshape(DEPTH, 1, d),
        "gwa": jnp.pad(gla_w_alpha.astype(BF16), ((0, 0), (0, LANES - GLA_RANK), (0, 0))),
        "gba": gla_b_alpha.reshape(DEPTH, 1, HDK),
        "gng": gla_norm_gain.reshape(DEPTH, 1, GLA_DV),
        "gwb": gla_w_branch.astype(BF16),
        "sccw": sc_conv_w,
        "scwb": sc_w_branch.astype(BF16),
        "rgcw": rg_conv_w,
        "rgcb": rg_conv_b.reshape(DEPTH, 1, d),
        "rgwax": jnp.concatenate([rg_w_a, rg_w_x], axis=-1).astype(BF16),
        "rgba": rg_b_a.reshape(DEPTH, 1, d),
        "rgbx": rg_b_x.reshape(DEPTH, 1, d),
        "rglam": rg_lambda.reshape(DEPTH, 1, d),
        "rgwb": rg_w_branch.astype(BF16),
        "bm": b_merge.reshape(DEPTH, 1, 3 * d),
        "wout": w_out.astype(BF16),
    }


def kernel(x_prompt, x_sample, c_prompt, c_sample, state_gla, state_sc_conv, state_rg_conv, state_rg_h, w_ada, b_ada, norm_gain, w_in, gla_w_alpha, gla_b_alpha, gla_norm_gain, gla_w_branch, sc_conv_w, sc_w_branch, rg_conv_w, rg_conv_b, rg_w_a, rg_b_a, rg_w_x, rg_b_x, rg_lambda, rg_w_branch, b_merge, w_out, final_gain):
    assert DEPTH == 2
    d = D_MODEL
    bp = x_prompt.shape[0]
    bs = x_sample.shape[0]
    fg = final_gain.reshape(1, d)
    mod_all = _ada_mod(jnp.concatenate([c_sample, c_prompt], axis=0), w_ada, b_ada)
    p = _stacked_params(norm_gain, w_in, gla_w_alpha, gla_b_alpha, gla_norm_gain, gla_w_branch,
                        sc_conv_w, sc_w_branch, rg_conv_w, rg_conv_b, rg_w_a, rg_b_a, rg_w_x, rg_b_x,
                        rg_lambda, rg_w_branch, b_merge, w_out)
    ssc = state_sc_conv.reshape(DEPTH, bs, (SC_CONV_W - 1) * d)
    srg = state_rg_conv.reshape(DEPTH, bs, (RG_CONV_W - 1) * d)

    xp = x_prompt
    xs = x_sample.reshape(bs, d)
    gla_p, sc_p, rgc_p, rgh_p = [], [], [], []
    gla_s, sc_s, rgc_s, rgh_s = None, [], [], []
    for l in range(DEPTH):
        final = l == DEPTH - 1
        xp, g, sc, rc, rh = _prompt_layer(xp, mod_all, bs, p, fg, l, final)
        gla_p.append(g)
        sc_p.append(sc)
        rgc_p.append(rc)
        rgh_p.append(rh.reshape(bp, d))

        q, k, v, ea = _sample_gla_prep(xs, mod_all, p, l)
        o, gla_s = _sample_gla_state(q, k, ea, v, state_gla, l, gla_s)
        xs, nsc, nrg, nrh = _sample_tail(xs, mod_all, o, ssc, srg, state_rg_h, p, fg, l, final)
        sc_s.append(nsc.reshape(bs, SC_CONV_W - 1, d))
        rgc_s.append(nrg.reshape(bs, RG_CONV_W - 1, d))
        rgh_s.append(nrh)

    return (xp, xs.reshape(bs, 1, d),
            jnp.stack(gla_p), jnp.stack(sc_p), jnp.stack(rgc_p), jnp.stack(rgh_p),
            gla_s, jnp.stack(sc_s), jnp.stack(rgc_s), jnp.stack(rgh_s))
```

```python
import functools

import jax
import jax.numpy as jnp
from jax import lax
from jax.experimental import pallas as pl
from jax.experimental.pallas import tpu as pltpu

F32 = jnp.float32
BF16 = jnp.bfloat16

D_MODEL = 1024
DEPTH = 2
GLA_HEADS = 4
GLA_DK = 128
GLA_DV = 256
GLA_RANK = 16
GLA_GATE_NORM = 16.0
HDK = GLA_HEADS * GLA_DK
HDV = GLA_HEADS * GLA_DV
SC_CONV_W = 3
RG_CONV_W = 4
RG_BLOCKS = 8
RG_BLOCK = 128
RG_C = 8.0
RMS_EPS = 1e-6

LANES = 128
SUBLANES = 8
ROW_TILE = 512
GLA_CHUNK = 64
GLA_HALF = GLA_CHUNK // 2
COL_BLOCK = 256
SEG_STEPS = ROW_TILE // SUBLANES
SLAB_PITCH = SEG_STEPS + 4
STATE_BATCH_TILE = 8
VMEM_LIMIT = 62 * 1024 * 1024

OFF_GLA = 0
OFF_LR = OFF_GLA + 2 * HDK + 2 * HDV
OFF_SC = OFF_LR + GLA_RANK
OFF_RG = OFF_SC + 4 * D_MODEL
OFF_MERGE = OFF_RG + 2 * D_MODEL
N_IN = OFF_MERGE + 3 * D_MODEL


def _mm(a, w):
    return jnp.dot(a.astype(BF16), w, preferred_element_type=F32)


def _mm_nt(a, b):
    return lax.dot_general(a.astype(BF16), b.astype(BF16), (((1,), (1,)), ((), ())),
                           preferred_element_type=F32)


def _mm_tn(a, b):
    return lax.dot_general(a.astype(BF16), b.astype(BF16), (((0,), (0,)), ((), ())),
                           preferred_element_type=F32)


def _rms(x, g):
    ms = jnp.mean(x * x, axis=-1, keepdims=True)
    return x * lax.rsqrt(ms + RMS_EPS) * g


def _silu(x):
    return x * jax.nn.sigmoid(x)


def _mod_norm(x, mod, g):
    d = D_MODEL
    shift, scale = mod[:, 0:d], mod[:, d:2 * d]
    return _rms(x, g) * (1.0 + scale) + shift


def _gla_log_decay(hb, wlr_ref, gwa_ref, gba_ref):
    zlr = _mm(hb, wlr_ref[...])
    pre = _mm(zlr, gwa_ref[...]) + gba_ref[...]
    return jax.nn.log_sigmoid(pre) * (1.0 / GLA_GATE_NORM)


def _gla_out_norm_gate(o, zg, gng):
    parts = []
    for h in range(GLA_HEADS):
        sl = slice(h * GLA_DV, (h + 1) * GLA_DV)
        parts.append(_rms(o[:, sl], gng) * _silu(zg[:, sl]))
    return parts


def _col(cb, base=0):
    return slice(base + cb * COL_BLOCK, base + (cb + 1) * COL_BLOCK)


def _acc(total, part):
    return part if total is None else total + part


def _rg_gate_logits(xc, cb, rgwax_ref):
    xb = xc.astype(BF16)
    per_cb = COL_BLOCK // RG_BLOCK
    r_parts, i_parts = [], []
    for n in range(per_cb):
        ri = jnp.dot(xb[:, n * RG_BLOCK:(n + 1) * RG_BLOCK], rgwax_ref[cb * per_cb + n],
                     preferred_element_type=F32)
        r_parts.append(ri[:, 0:RG_BLOCK])
        i_parts.append(ri[:, RG_BLOCK:2 * RG_BLOCK])
    return jnp.concatenate(r_parts, axis=1), jnp.concatenate(i_parts, axis=1)


def _rg_coeffs(xc, r_pre, i_pre, cb, ba_ref, bx_ref, lam_ref):
    cs = _col(cb)
    r = jax.nn.sigmoid(r_pre + ba_ref[:, cs])
    ig = jax.nn.sigmoid(i_pre + bx_ref[:, cs])
    lam = lam_ref[:, cs]
    softplus_neg_lam = jnp.log1p(jnp.exp(-jnp.abs(lam))) + jnp.maximum(-lam, 0.0)
    log_a = (-RG_C) * r * softplus_neg_lam
    a = jnp.exp(log_a)
    rad = 1.0 - a * a
    root = jnp.where(rad > 0.0, rad * lax.rsqrt(rad), 0.0)
    bterm = root * (ig * xc)
    return a, bterm


def _merge_out(x, gate, ys, logits, bm_ref, wout_ref):
    d = D_MODEL
    mix = None
    for n, (y, m) in enumerate(zip(ys, logits)):
        mix = _acc(mix, jax.nn.sigmoid(m + bm_ref[:, n * d:(n + 1) * d]) * y)
    return x + gate * _mm(mix, wout_ref[...])


def _ada_body(c_ref, w_ref, b_ref, o_ref):
    o_ref[...] = _mm(_silu(c_ref[...]), w_ref[...].astype(BF16)) + b_ref[...]


def _ada_mod(c_all, w_ada, b_ada):
    rows = c_all.shape[0]
    d = D_MODEL
    return pl.pallas_call(
        _ada_body,
        grid=(DEPTH, 3),
        in_specs=[pl.BlockSpec((rows, d), lambda l, j: (0, 0)),
                  pl.BlockSpec((None, d, d), lambda l, j: (l, 0, j)),
                  pl.BlockSpec((None, 1, d), lambda l, j: (l, 0, j))],
        out_specs=pl.BlockSpec((None, rows, d), lambda l, j: (l, 0, j)),
        out_shape=jax.ShapeDtypeStruct((DEPTH, rows, 3 * d), F32),
        name="ada_mod",
    )(c_all, w_ada, b_ada.reshape(DEPTH, 1, 3 * d))


W_IN_PIECES = (("wgla", OFF_GLA, OFF_LR), ("wsc", OFF_SC, OFF_RG), ("wrg", OFF_RG, OFF_MERGE),
               ("wm", OFF_MERGE, N_IN))
W_IN_SPLIT_BLOCK = 1024


def _split_piece_body(wt_ref, o_ref, *, valid):
    w = wt_ref[0].T
    if valid < w.shape[1]:
        lane = lax.broadcasted_iota(jnp.int32, w.shape, 1)
        w = jnp.where(lane < valid, w, 0.0)
    o_ref[...] = w.astype(BF16)


def _split_piece(w_in_t, lo, width, block, name):
    d = D_MODEL
    n_blocks = -(-width // block)
    in_block = (pl.Element(1), pl.Element(block), pl.Element(d))
    return pl.pallas_call(
        functools.partial(_split_piece_body, valid=min(width, block)),
        grid=(DEPTH, n_blocks),
        in_specs=[pl.BlockSpec(in_block, lambda l, i: (l, pl.multiple_of(lo + block * i, 16), 0))],
        out_specs=pl.BlockSpec((None, d, block), lambda l, i: (l, 0, i)),
        out_shape=jax.ShapeDtypeStruct((DEPTH, d, n_blocks * block), BF16),
        compiler_params=pltpu.CompilerParams(vmem_limit_bytes=VMEM_LIMIT),
        name=name,
    )(w_in_t)


def _split_w_in(w_in):
    w_in_t = jnp.swapaxes(w_in, 1, 2)
    pieces = {name: _split_piece(w_in_t, lo, hi - lo, W_IN_SPLIT_BLOCK, "split_" + name)
              for name, lo, hi in W_IN_PIECES}
    pieces["wlr"] = _split_piece(w_in_t, OFF_LR, GLA_RANK, LANES, "split_wlr")
    return pieces


def _prompt_layer_body(x_ref, mod_ref, ng_ref, wgla_ref, wlr_ref, gwa_ref, gba_ref, gng_ref, gwb_ref,
                       wsc_ref, sccw_ref, scwb_ref, wrg_ref, rgcw_ref, rgcb_ref, rgwax_ref, rgba_ref,
                       rgbx_ref, rglam_ref, rgwb_ref, wm_ref, bm_ref, wout_ref, fg_ref,
                       y_ref, gla_ref, sct_ref, rgt_ref, rgh_ref,
                       st_ref, hcar_ref, slab_h_ref, slab_g_ref, slab_o_ref, *, final):
    d = D_MODEL
    i = pl.program_id(1)

    @pl.when(i == 0)
    def _():
        st_ref[...] = jnp.zeros(st_ref.shape, F32)
        sct_ref[...] = jnp.zeros(sct_ref.shape, F32)
        rgt_ref[...] = jnp.zeros(rgt_ref.shape, F32)
        hcar_ref[...] = jnp.zeros(hcar_ref.shape, F32)

    x = x_ref[...]
    mod = mod_ref[...]
    gate = mod[:, 2 * d:3 * d]
    h = _mod_norm(x, mod, ng_ref[...])
    hb = h.astype(BF16)
    hb_s = _to_segment_major(h, slab_h_ref).astype(BF16)

    y_rg = _from_segment_major(
        _prompt_rg_branch(hb_s, wrg_ref, rgcw_ref, rgcb_ref, rgwax_ref, rgba_ref, rgbx_ref, rglam_ref,
                          rgwb_ref, hcar_ref, rgt_ref, rgh_ref), slab_o_ref)
    t_gla = _prompt_gla_branch(hb, wgla_ref, wlr_ref, gwa_ref, gba_ref, gng_ref, st_ref)
    y_gla = None
    for hd in range(GLA_HEADS):
        y_gla = _acc(y_gla, _mm(t_gla[hd], gwb_ref[hd * GLA_DV:(hd + 1) * GLA_DV, :]))
    y_sc = _from_segment_major(
        _mm(_prompt_sc_branch(hb_s, wsc_ref, sccw_ref, sct_ref), scwb_ref[...]), slab_g_ref)
    logits = tuple(_mm(hb, wm_ref[:, n * d:(n + 1) * d]) for n in range(3))
    xn = _merge_out(x, gate, (y_gla, y_sc, y_rg), logits, bm_ref, wout_ref)
    if final:
        xn = _rms(xn, fg_ref[...])
    y_ref[...] = xn

    @pl.when(i == pl.num_programs(1) - 1)
    def _():
        for hd in range(GLA_HEADS):
            gla_ref[hd] = st_ref[hd].T


def _to_segment_major(v, slab_ref):
    for lb in range(D_MODEL // LANES):
        for j in range(SUBLANES):
            slab_ref[lb, j * SLAB_PITCH:j * SLAB_PITCH + SEG_STEPS, :] = (
                v[j * SEG_STEPS:(j + 1) * SEG_STEPS, lb * LANES:(lb + 1) * LANES])
    steps = []
    for s in range(SEG_STEPS):
        steps.append(jnp.concatenate(
            [slab_ref[lb, pl.ds(s, SUBLANES, stride=SLAB_PITCH), :] for lb in range(D_MODEL // LANES)], axis=1))
    return jnp.concatenate(steps, axis=0)


def _from_segment_major(v, slab_ref):
    for s in range(SEG_STEPS):
        for lb in range(D_MODEL // LANES):
            slab_ref[lb, pl.ds(s, SUBLANES, stride=SLAB_PITCH), :] = (
                v[s * SUBLANES:(s + 1) * SUBLANES, lb * LANES:(lb + 1) * LANES])
    cols = []
    for lb in range(D_MODEL // LANES):
        cols.append(jnp.concatenate(
            [slab_ref[lb, j * SLAB_PITCH:j * SLAB_PITCH + SEG_STEPS, :] for j in range(SUBLANES)], axis=0))
    return jnp.concatenate(cols, axis=1)


def _seg_conv(v, w, tails, bias=None):
    k = w.shape[0] - 1
    width = v.shape[1]
    sub = lax.broadcasted_iota(jnp.int32, (SUBLANES, width), 0)
    out = v * w[k:k + 1]
    if bias is not None:
        out = out + bias
    for back in range(1, k + 1):
        heads = []
        for r in range(back):
            s = SEG_STEPS - back + r
            wrapped = pltpu.roll(v[s * SUBLANES:(s + 1) * SUBLANES, :], 1, 0)
            heads.append(jnp.where(sub == 0, tails[k - back + r:k - back + r + 1, :], wrapped))
        shifted = jnp.concatenate(heads + [v[0:(SEG_STEPS - back) * SUBLANES, :]], axis=0)
        out = out + shifted * w[k - back:k - back + 1]
    return out


def _seg_tails(v, k):
    return jnp.concatenate(
        [v[(SEG_STEPS - q) * SUBLANES + SUBLANES - 1:(SEG_STEPS - q + 1) * SUBLANES, :] for q in range(k, 0, -1)],
        axis=0)


def _seg_scan(a, b, h0):
    def step(s, v):
        return v[s * SUBLANES:(s + 1) * SUBLANES, :]
    prod, loc = step(0, a), step(0, b)
    for s in range(1, SEG_STEPS):
        loc = step(s, a) * loc + step(s, b)
        prod = prod * step(s, a)
    h_in = h0
    entering = []
    for j in range(SUBLANES):
        entering.append(h_in)
        h_in = prod[j:j + 1] * h_in + loc[j:j + 1]
    cur = jnp.concatenate(entering, axis=0)
    outs = []
    for s in range(SEG_STEPS):
        cur = step(s, a) * cur + step(s, b)
        outs.append(cur)
    return jnp.concatenate(outs, axis=0), h_in


def _prompt_gla_branch(hb, wgla_ref, wlr_ref, gwa_ref, gba_ref, gng_ref, st_ref):
    tt = ROW_TILE
    zlr = _mm(hb, wlr_ref[...])
    zq = _mm(hb, wgla_ref[:, 0:HDK]) * (GLA_DK ** -0.5)
    pre = _mm(zlr, gwa_ref[...]) + gba_ref[...]
    zk = _mm(hb, wgla_ref[:, HDK:2 * HDK])
    la = jax.nn.log_sigmoid(pre) * (1.0 / GLA_GATE_NORM)
    row = lax.broadcasted_iota(jnp.int32, (tt, tt), 0)
    col = lax.broadcasted_iota(jnp.int32, (tt, tt), 1)
    chunk_bits = GLA_CHUNK.bit_length() - 1
    causal = jnp.logical_and(jnp.right_shift(row, chunk_bits) == jnp.right_shift(col, chunk_bits), col <= row)
    tri = jnp.where(causal, 1.0, 0.0).astype(BF16)
    la_hi = la.astype(BF16)
    la_lo = (la - la_hi.astype(F32)).astype(BF16)
    bcum = (jnp.dot(tri, la_hi, preferred_element_type=F32)
            + jnp.dot(tri, la_lo, preferred_element_type=F32))
    zv = _mm(hb, wgla_ref[:, 2 * HDK:2 * HDK + HDV])
    zg = _mm(hb, wgla_ref[:, 2 * HDK + HDV:2 * HDK + 2 * HDV])
    gng = gng_ref[...]
    n_chunk = tt // GLA_CHUNK
    gated = []
    for h in range(GLA_HEADS):
        ks = slice(h * GLA_DK, (h + 1) * GLA_DK)
        vs = slice(h * GLA_DV, (h + 1) * GLA_DV)
        b = bcum[:, ks]
        b3 = b.reshape(n_chunk, GLA_CHUNK, GLA_DK)
        b_mid = jnp.broadcast_to(b3[:, GLA_HALF - 1:GLA_HALF, :], b3.shape).reshape(tt, GLA_DK)
        b_end = jnp.broadcast_to(b3[:, GLA_CHUNK - 1:GLA_CHUNK, :], b3.shape).reshape(tt, GLA_DK)
        q = zq[:, ks]
        k = zk[:, ks]
        qe = q * jnp.exp(b - b_mid)
        ke = k * jnp.exp(b_mid - b)
        qs = (q * jnp.exp(b)).astype(BF16)
        kd = (k * jnp.exp(b_end - b)).astype(BF16)
        att = jnp.where(causal, _mm_nt(qe, ke), 0.0)
        vb = zv[:, vs].astype(BF16)
        o = jnp.dot(att.astype(BF16), vb, preferred_element_type=F32)
        st = st_ref[h]
        o_parts = []
        for c in range(n_chunk):
            rs = slice(c * GLA_CHUNK, (c + 1) * GLA_CHUNK)
            o_parts.append(o[rs] + _mm_nt(qs[rs], st))
            e_end = jnp.exp(b3[c, GLA_CHUNK - 1:GLA_CHUNK, :])
            st = st * e_end + _mm_tn(vb[rs], kd[rs])
        st_ref[h] = st
        gated.append(_rms(jnp.concatenate(o_parts, axis=0), gng) * _silu(zg[:, vs]))
    return gated


def _prompt_sc_branch(hb, wsc_ref, sccw_ref, sct_ref):
    d = D_MODEL
    u = _mm(hb, wsc_ref[:, d:2 * d]) * _mm(hb, wsc_ref[:, 2 * d:3 * d])
    uc = _seg_conv(u, sccw_ref[...], sct_ref[...])
    sct_ref[...] = _seg_tails(u, SC_CONV_W - 1)
    zb = _mm(hb, wsc_ref[:, 0:d])
    zgs = _mm(hb, wsc_ref[:, 3 * d:4 * d])
    return zb * uc * _silu(zgs)


def _prompt_rg_branch(hb, wrg_ref, rgcw_ref, rgcb_ref, rgwax_ref, rgba_ref, rgbx_ref, rglam_ref, rgwb_ref,
                      hcar_ref, rgt_ref, rgh_ref):
    d = D_MODEL
    n_cb = d // COL_BLOCK
    rw = rgcw_ref[...]
    old_tails = rgt_ref[...]

    def project(cb):
        return _mm(hb, wrg_ref[:, _col(cb)]), _mm(hb, wrg_ref[:, _col(cb, d)])

    def conv_and_gate_logits(cb, zx):
        cs = _col(cb)
        xc = _seg_conv(zx, rw[:, cs], old_tails[:, cs], rgcb_ref[:, cs])
        rgt_ref[:, cs] = _seg_tails(zx, RG_CONV_W - 1)
        return (xc,) + _rg_gate_logits(xc, cb, rgwax_ref)

    def recur(cb, xc, r_pre, i_pre, zgr):
        cs = _col(cb)
        a, bt = _rg_coeffs(xc, r_pre, i_pre, cb, rgba_ref, rgbx_ref, rglam_ref)
        h_all, h_last = _seg_scan(a, bt, hcar_ref[:, cs])
        hcar_ref[:, cs] = h_last
        rgh_ref[:, cs] = h_last
        return h_all * _silu(zgr)

    proj = {0: project(0), 1: project(1)}
    logits = {0: conv_and_gate_logits(0, proj[0][0])}
    y = None
    for cb in range(n_cb):
        if cb + 2 < n_cb:
            proj[cb + 2] = project(cb + 2)
        if cb + 1 < n_cb:
            logits[cb + 1] = conv_and_gate_logits(cb + 1, proj[cb + 1][0])
        t = recur(cb, *logits[cb], proj[cb][1])
        y = _acc(y, _mm(t, rgwb_ref[_col(cb), :]))
    return y


def _layer_spec(stacked, layer):
    rest = stacked.shape[1:]
    return pl.BlockSpec((None,) + rest, lambda *_: (layer,) + (0,) * len(rest), pipeline_mode=pl.Buffered(1))


def _whole_spec(arr):
    return pl.BlockSpec(arr.shape, lambda *_: (0,) * arr.ndim, pipeline_mode=pl.Buffered(1))


PROMPT_WEIGHTS = ("ng", "wgla", "wlr", "gwa", "gba", "gng", "gwb", "wsc", "sccw", "scwb", "wrg", "rgcw",
                  "rgcb", "rgwax", "rgba", "rgbx", "rglam", "rgwb", "wm", "bm", "wout")


def _prompt_layer(x, mod_all, mod_row0, p, final_gain, layer, final):
    bsz, seq, d = x.shape
    tt = ROW_TILE
    weights = tuple(p[name] for name in PROMPT_WEIGHTS)
    mod4 = mod_all.reshape(DEPTH, mod_all.shape[1], 1, 3 * d)
    in_specs = [pl.BlockSpec((None, tt, d), lambda b, i: (b, i, 0)),
                pl.BlockSpec((None, None, 1, 3 * d), lambda b, i: (layer, mod_row0 + b, 0, 0))]
    in_specs += [_layer_spec(w, layer) for w in weights] + [_whole_spec(final_gain)]
    out_specs = [pl.BlockSpec((None, tt, d), lambda b, i: (b, i, 0)),
                 pl.BlockSpec((None, GLA_HEADS, GLA_DK, GLA_DV), lambda b, i: (b, 0, 0, 0)),
                 pl.BlockSpec((None, SC_CONV_W - 1, d), lambda b, i: (b, 0, 0)),
                 pl.BlockSpec((None, RG_CONV_W - 1, d), lambda b, i: (b, 0, 0)),
                 pl.BlockSpec((None, 1, d), lambda b, i: (b, 0, 0))]
    out_shape = [jax.ShapeDtypeStruct((bsz, seq, d), F32),
                 jax.ShapeDtypeStruct((bsz, GLA_HEADS, GLA_DK, GLA_DV), F32),
                 jax.ShapeDtypeStruct((bsz, SC_CONV_W - 1, d), F32),
                 jax.ShapeDtypeStruct((bsz, RG_CONV_W - 1, d), F32),
                 jax.ShapeDtypeStruct((bsz, 1, d), F32)]
    scratch = [pltpu.VMEM((GLA_HEADS, GLA_DV, GLA_DK), F32),
               pltpu.VMEM((1, d), F32)]
    scratch += [pltpu.VMEM((d // LANES, SUBLANES * SLAB_PITCH, LANES), F32)] * 3
    return pl.pallas_call(
        functools.partial(_prompt_layer_body, final=final),
        grid=(bsz, seq // tt),
        in_specs=in_specs,
        out_specs=out_specs,
        out_shape=out_shape,
        scratch_shapes=scratch,
        compiler_params=pltpu.CompilerParams(
            dimension_semantics=("arbitrary", "arbitrary"), vmem_limit_bytes=VMEM_LIMIT),
        name="prompt_layer",
    )(x, mod4, *weights, final_gain)


def _sample_gla_prep_body(x_ref, mod_ref, ng_ref, wgla_ref, wlr_ref, gwa_ref, gba_ref,
                          q_ref, k_ref, v_ref, ea_ref):
    hb = _mod_norm(x_ref[...], mod_ref[...], ng_ref[...]).astype(BF16)
    la = _gla_log_decay(hb, wlr_ref, gwa_ref, gba_ref)
    ea_ref[...] = jnp.exp(la)
    q_ref[...] = _mm(hb, wgla_ref[:, 0:HDK]) * (GLA_DK ** -0.5)
    k_ref[...] = _mm(hb, wgla_ref[:, HDK:2 * HDK])
    v_ref[...] = _mm(hb, wgla_ref[:, 2 * HDK:2 * HDK + HDV])


def _sample_mod_spec(rows, layer):
    return pl.BlockSpec((None, rows, 3 * D_MODEL), lambda *_: (layer, 0, 0))


def _sample_gla_prep(x, mod_all, p, layer):
    rows = x.shape[0]
    weights = tuple(p[name] for name in ("ng", "wgla", "wlr", "gwa", "gba"))
    out_shape = [jax.ShapeDtypeStruct((rows, HDK), F32),
                 jax.ShapeDtypeStruct((rows, HDK), F32),
                 jax.ShapeDtypeStruct((rows, HDV), F32),
                 jax.ShapeDtypeStruct((rows, HDK), F32)]
    return pl.pallas_call(
        _sample_gla_prep_body,
        grid=(1,),
        in_specs=[_whole_spec(x), _sample_mod_spec(rows, layer)] + [_layer_spec(w, layer) for w in weights],
        out_specs=[pl.BlockSpec(s.shape, lambda i: (0, 0)) for s in out_shape],
        out_shape=out_shape,
        compiler_params=pltpu.CompilerParams(vmem_limit_bytes=VMEM_LIMIT),
        name="sample_gla_prep",
    )(x, mod_all, *weights)


def _sample_gla_state_body(q_ref, k_ref, ea_ref, v_ref, s_ref, *rest):
    o_ref, so_ref = rest[-2:]
    for h in range(GLA_HEADS):
        ks = slice(h * GLA_DK, (h + 1) * GLA_DK)
        vs = slice(h * GLA_DV, (h + 1) * GLA_DV)
        q_t = q_ref[:, ks].T
        k_t = k_ref[:, ks].T
        e_t = ea_ref[:, ks].T
        for j in range(STATE_BATCH_TILE):
            s_new = e_t[:, j:j + 1] * s_ref[j, h] + k_t[:, j:j + 1] * v_ref[j:j + 1, vs]
            so_ref[j, h] = s_new
            o_ref[j:j + 1, vs] = jnp.sum(q_t[:, j:j + 1] * s_new, axis=0, keepdims=True)


def _sample_gla_state(q, k, ea, v, state, layer, carrier):
    rows = q.shape[0]
    bt = STATE_BATCH_TILE
    state_block = (None, bt, GLA_HEADS, GLA_DK, GLA_DV)
    state_index = lambda i: (layer, i, 0, 0, 0)
    args = [q, k, ea, v, state]
    in_specs = [pl.BlockSpec((bt, HDK), lambda i: (i, 0)),
                pl.BlockSpec((bt, HDK), lambda i: (i, 0)),
                pl.BlockSpec((bt, HDK), lambda i: (i, 0)),
                pl.BlockSpec((bt, HDV), lambda i: (i, 0)),
                pl.BlockSpec(state_block, state_index)]
    aliases = {}
    if carrier is not None:
        args.append(carrier)
        in_specs.append(pl.BlockSpec(memory_space=pl.ANY))
        aliases = {len(args) - 1: 1}
    return pl.pallas_call(
        _sample_gla_state_body,
        grid=(rows // bt,),
        in_specs=in_specs,
        out_specs=[pl.BlockSpec((bt, HDV), lambda i: (i, 0)),
                   pl.BlockSpec(state_block, state_index)],
        out_shape=[jax.ShapeDtypeStruct((rows, HDV), F32),
                   jax.ShapeDtypeStruct(state.shape, F32)],
        input_output_aliases=aliases,
        name="sample_gla_state",
    )(*args)


def _sample_tail_body(x_ref, mod_ref, o_ref, ssc_ref, srg_ref, srh_ref,
                      ng_ref, wgla_ref, gng_ref, gwb_ref, wsc_ref, sccw_ref, scwb_ref, wrg_ref, rgcw_ref,
                      rgcb_ref, rgwax_ref, rgba_ref, rgbx_ref, rglam_ref, rgwb_ref, wm_ref, bm_ref, wout_ref,
                      fg_ref, y_ref, nsc_ref, nrg_ref, nrh_ref, *, final):
    d = D_MODEL
    x = x_ref[...]
    mod = mod_ref[...]
    gate = mod[:, 2 * d:3 * d]
    hb = _mod_norm(x, mod, ng_ref[...]).astype(BF16)

    zg = _mm(hb, wgla_ref[:, 2 * HDK + HDV:2 * HDK + 2 * HDV])
    parts = _gla_out_norm_gate(o_ref[...], zg, gng_ref[...])
    y_gla = _mm(jnp.concatenate(parts, axis=1), gwb_ref[...])

    u = _mm(hb, wsc_ref[:, d:2 * d]) * _mm(hb, wsc_ref[:, 2 * d:3 * d])
    cw = sccw_ref[...]
    s0, s1 = ssc_ref[:, 0:d], ssc_ref[:, d:2 * d]
    uc = s0 * cw[0:1] + s1 * cw[1:2] + u * cw[2:3]
    nsc_ref[:, 0:d] = s1
    nsc_ref[:, d:2 * d] = u
    zb = _mm(hb, wsc_ref[:, 0:d])
    zgs = _mm(hb, wsc_ref[:, 3 * d:4 * d])
    y_sc = _mm(zb * uc * _silu(zgs), scwb_ref[...])

    zx = _mm(hb, wrg_ref[:, 0:d])
    rw = rgcw_ref[...]
    r0, r1, r2 = srg_ref[:, 0:d], srg_ref[:, d:2 * d], srg_ref[:, 2 * d:3 * d]
    xc = r0 * rw[0:1] + r1 * rw[1:2] + r2 * rw[2:3] + zx * rw[3:4] + rgcb_ref[...]
    nrg_ref[:, 0:d] = r1
    nrg_ref[:, d:2 * d] = r2
    nrg_ref[:, 2 * d:3 * d] = zx
    h_parts = []
    for cb in range(d // COL_BLOCK):
        cs = _col(cb)
        r_pre, i_pre = _rg_gate_logits(xc[:, cs], cb, rgwax_ref)
        a, bt = _rg_coeffs(xc[:, cs], r_pre, i_pre, cb, rgba_ref, rgbx_ref, rglam_ref)
        h_parts.append(a * srh_ref[:, cs] + bt)
    hr = jnp.concatenate(h_parts, axis=1)
    nrh_ref[...] = hr
    zgr = _mm(hb, wrg_ref[:, d:2 * d])
    y_rg = _mm(hr * _silu(zgr), rgwb_ref[...])

    logits = tuple(_mm(hb, wm_ref[:, n * d:(n + 1) * d]) for n in range(3))
    xn = _merge_out(x, gate, (y_gla, y_sc, y_rg), logits, bm_ref, wout_ref)
    if final:
        xn = _rms(xn, fg_ref[...])
    y_ref[...] = xn


def _sample_tail(x, mod_all, o, ssc, srg, srh, p, final_gain, layer, final):
    rows, d = x.shape
    weights = tuple(p[name] for name in ("ng", "wgla", "gng", "gwb", "wsc", "sccw", "scwb", "wrg", "rgcw",
                                         "rgcb", "rgwax", "rgba", "rgbx", "rglam", "rgwb", "wm", "bm", "wout"))
    out_shape = [jax.ShapeDtypeStruct((rows, d), F32),
                 jax.ShapeDtypeStruct((rows, (SC_CONV_W - 1) * d), F32),
                 jax.ShapeDtypeStruct((rows, (RG_CONV_W - 1) * d), F32),
                 jax.ShapeDtypeStruct((rows, d), F32)]
    in_specs = [_whole_spec(x), _sample_mod_spec(rows, layer), _whole_spec(o),
                _layer_spec(ssc, layer), _layer_spec(srg, layer), _layer_spec(srh, layer)]
    in_specs += [_layer_spec(w, layer) for w in weights] + [_whole_spec(final_gain)]
    return pl.pallas_call(
        functools.partial(_sample_tail_body, final=final),
        grid=(1,),
        in_specs=in_specs,
        out_specs=[pl.BlockSpec(s.shape, lambda i: (0, 0)) for s in out_shape],
        out_shape=out_shape,
        compiler_params=pltpu.CompilerParams(vmem_limit_bytes=VMEM_LIMIT),
        name="sample_tail",
    )(x, mod_all, o, ssc, srg, srh, *weights, final_gain)


def _stacked_params(norm_gain, w_in, gla_w_alpha, gla_b_alpha, gla_norm_gain, gla_w_branch, sc_conv_w,
                    sc_w_branch, rg_conv_w, rg_conv_b, rg_w_a, rg_b_a, rg_w_x, rg_b_x, rg_lambda,
                    rg_w_branch, b_merge, w_out):
    d = D_MODEL
    return {
        **_split_w_in(w_in),
        "ng": norm_gain.reshape(DEPTH, 1, d),
        "gwa": jnp.pad(gla_w_alpha.astype(BF16), ((0, 0), (0, LANES - GLA_RANK), (0, 0))),
        "gba": gla_b_alpha.reshape(DEPTH, 1, HDK),
        "gng": gla_norm_gain.reshape(DEPTH, 1, GLA_DV),
        "gwb": gla_w_branch.astype(BF16),
        "sccw": sc_conv_w,
        "scwb": sc_w_branch.astype(BF16),
        "rgcw": rg_conv_w,
        "rgcb": rg_conv_b.reshape(DEPTH, 1, d),
        "rgwax": jnp.concatenate([rg_w_a, rg_w_x], axis=-1).astype(BF16),
        "rgba": rg_b_a.reshape(DEPTH, 1, d),
        "rgbx": rg_b_x.reshape(DEPTH, 1, d),
        "rglam": rg_lambda.reshape(DEPTH, 1, d),
        "rgwb": rg_w_branch.astype(BF16),
        "bm": b_merge.reshape(DEPTH, 1, 3 * d),
        "wout": w_out.astype(BF16),
    }


def kernel(x_prompt, x_sample, c_prompt, c_sample, state_gla, state_sc_conv, state_rg_conv, state_rg_h, w_ada, b_ada, norm_gain, w_in, gla_w_alpha, gla_b_alpha, gla_norm_gain, gla_w_branch, sc_conv_w, sc_w_branch, rg_conv_w, rg_conv_b, rg_w_a, rg_b_a, rg_w_x, rg_b_x, rg_lambda, rg_w_branch, b_merge, w_out, final_gain):
    assert DEPTH == 2
    d = D_MODEL
    bp = x_prompt.shape[0]
    bs = x_sample.shape[0]
    fg = final_gain.reshape(1, d)
    mod_all = _ada_mod(jnp.concatenate([c_sample, c_prompt], axis=0), w_ada, b_ada)
    p = _stacked_params(norm_gain, w_in, gla_w_alpha, gla_b_alpha, gla_norm_gain, gla_w_branch,
                        sc_conv_w, sc_w_branch, rg_conv_w, rg_conv_b, rg_w_a, rg_b_a, rg_w_x, rg_b_x,
                        rg_lambda, rg_w_branch, b_merge, w_out)
    ssc = state_sc_conv.reshape(DEPTH, bs, (SC_CONV_W - 1) * d)
    srg = state_rg_conv.reshape(DEPTH, bs, (RG_CONV_W - 1) * d)

    xp = x_prompt
    xs = x_sample.reshape(bs, d)
    gla_p, sc_p, rgc_p, rgh_p = [], [], [], []
    gla_s, sc_s, rgc_s, rgh_s = None, [], [], []
    for l in range(DEPTH):
        final = l == DEPTH - 1
        xp, g, sc, rc, rh = _prompt_layer(xp, mod_all, bs, p, fg, l, final)
        gla_p.append(g)
        sc_p.append(sc)
        rgc_p.append(rc)
        rgh_p.append(rh.reshape(bp, d))

        q, k, v, ea = _sample_gla_prep(xs, mod_all, p, l)
        o, gla_s = _sample_gla_state(q, k, ea, v, state_gla, l, gla_s)
        xs, nsc, nrg, nrh = _sample_tail(xs, mod_all, o, ssc, srg, state_rg_h, p, fg, l, final)
        sc_s.append(nsc.reshape(bs, SC_CONV_W - 1, d))
        rgc_s.append(nrg.reshape(bs, RG_CONV_W - 1, d))
        rgh_s.append(nrh)

    return (xp, xs.reshape(bs, 1, d),
            jnp.stack(gla_p), jnp.stack(sc_p), jnp.stack(rgc_p), jnp.stack(rgh_p),
            gla_s, jnp.stack(sc_s), jnp.stack(rgc_s), jnp.stack(rgh_s))
```

```python
import functools

import jax
import jax.numpy as jnp
from jax import lax
from jax.experimental import pallas as pl
from jax.experimental.pallas import tpu as pltpu

F32 = jnp.float32
BF16 = jnp.bfloat16

D_MODEL = 1024
DEPTH = 2
GLA_HEADS = 4
GLA_DK = 128
GLA_DV = 256
GLA_RANK = 16
GLA_GATE_NORM = 16.0
HDK = GLA_HEADS * GLA_DK
HDV = GLA_HEADS * GLA_DV
SC_CONV_W = 3
RG_CONV_W = 4
RG_BLOCKS = 8
RG_BLOCK = 128
RG_C = 8.0
RMS_EPS = 1e-6
LOG2_E = 1.4426950408889634

LANES = 128
SUBLANES = 8
ROW_TILE = 512
GLA_CHUNK = 64
GLA_HALF = GLA_CHUNK // 2
GLA_MASK_BLOCK = 256
COL_BLOCK = 256
SEG_STEPS = ROW_TILE // SUBLANES
SLAB_PITCH = SEG_STEPS + 4
STATE_BATCH_TILE = 16
VMEM_LIMIT = 62 * 1024 * 1024

OFF_GLA = 0
OFF_LR = OFF_GLA + 2 * HDK + 2 * HDV
OFF_SC = OFF_LR + GLA_RANK
OFF_RG = OFF_SC + 4 * D_MODEL
OFF_MERGE = OFF_RG + 2 * D_MODEL
N_IN = OFF_MERGE + 3 * D_MODEL


def _mm(a, w):
    return jnp.dot(a.astype(BF16), w, preferred_element_type=F32)


def _mm_nt(a, b):
    return lax.dot_general(a.astype(BF16), b.astype(BF16), (((1,), (1,)), ((), ())),
                           preferred_element_type=F32)


def _mm_tn(a, b):
    return lax.dot_general(a.astype(BF16), b.astype(BF16), (((0,), (0,)), ((), ())),
                           preferred_element_type=F32)


def _rms(x, g):
    ms = jnp.mean(x * x, axis=-1, keepdims=True)
    return x * lax.rsqrt(ms + RMS_EPS) * g


def _silu(x):
    return x * jax.nn.sigmoid(x)


def _mod_norm(x, mod, g):
    d = D_MODEL
    shift, scale = mod[:, 0:d], mod[:, d:2 * d]
    return _rms(x, g) * (1.0 + scale) + shift


def _gla_log_decay(hb, wlr_ref, gwa_ref, gba_ref):
    zlr = _mm(hb, wlr_ref[...])
    pre = _mm(zlr, gwa_ref[...]) + gba_ref[...]
    return jax.nn.log_sigmoid(pre) * (1.0 / GLA_GATE_NORM)


def _gla_out_norm_gate(o, zg, gng):
    parts = []
    for h in range(GLA_HEADS):
        sl = slice(h * GLA_DV, (h + 1) * GLA_DV)
        parts.append(_rms(o[:, sl], gng) * _silu(zg[:, sl]))
    return parts


def _col(cb, base=0):
    return slice(base + cb * COL_BLOCK, base + (cb + 1) * COL_BLOCK)


def _acc(total, part):
    return part if total is None else total + part


def _rg_gate_logits(xc, cb, rgwax_ref):
    xb = xc.astype(BF16)
    per_cb = COL_BLOCK // RG_BLOCK
    r_parts, i_parts = [], []
    for n in range(per_cb):
        ri = jnp.dot(xb[:, n * RG_BLOCK:(n + 1) * RG_BLOCK], rgwax_ref[cb * per_cb + n],
                     preferred_element_type=F32)
        r_parts.append(ri[:, 0:RG_BLOCK])
        i_parts.append(ri[:, RG_BLOCK:2 * RG_BLOCK])
    return jnp.concatenate(r_parts, axis=1), jnp.concatenate(i_parts, axis=1)


def _rg_coeffs(xc, r_pre, i_pre, cb, ba_ref, bx_ref, lam_ref):
    cs = _col(cb)
    r = jax.nn.sigmoid(r_pre + ba_ref[:, cs])
    ig = jax.nn.sigmoid(i_pre + bx_ref[:, cs])
    lam = lam_ref[:, cs]
    softplus_neg_lam = jnp.log1p(jnp.exp(-jnp.abs(lam))) + jnp.maximum(-lam, 0.0)
    a = jnp.exp2(r * (softplus_neg_lam * (-RG_C * LOG2_E)))
    rad = 1.0 - a * a
    root = jnp.where(rad > 0.0, rad * lax.rsqrt(rad), 0.0)
    bterm = root * (ig * xc)
    return a, bterm


def _merge_out(x, gate, ys, logits, bm_ref, wout_ref):
    d = D_MODEL
    mix = None
    for n, (y, m) in enumerate(zip(ys, logits)):
        mix = _acc(mix, jax.nn.sigmoid(m + bm_ref[:, n * d:(n + 1) * d]) * y)
    return x + gate * _mm(mix, wout_ref[...])


def _ada_body(c_ref, w_ref, b_ref, o_ref):
    o_ref[...] = _mm(_silu(c_ref[...]), w_ref[...].astype(BF16)) + b_ref[...]


def _ada_mod(c_all, w_ada, b_ada):
    rows = c_all.shape[0]
    d = D_MODEL
    return pl.pallas_call(
        _ada_body,
        grid=(DEPTH, 3),
        in_specs=[pl.BlockSpec((rows, d), lambda l, j: (0, 0)),
                  pl.BlockSpec((None, d, d), lambda l, j: (l, 0, j)),
                  pl.BlockSpec((None, 1, d), lambda l, j: (l, 0, j))],
        out_specs=pl.BlockSpec((None, rows, d), lambda l, j: (l, 0, j)),
        out_shape=jax.ShapeDtypeStruct((DEPTH, rows, 3 * d), F32),
        name="ada_mod",
    )(c_all, w_ada, b_ada.reshape(DEPTH, 1, 3 * d))


W_IN_PIECES = (("wgla", OFF_GLA, OFF_LR), ("wsc", OFF_SC, OFF_RG), ("wrg", OFF_RG, OFF_MERGE),
               ("wm", OFF_MERGE, N_IN))
W_IN_SPLIT_BLOCK = 1024


def _split_piece_body(wt_ref, o_ref, *, valid):
    w = wt_ref[0].T
    if valid < w.shape[1]:
        lane = lax.broadcasted_iota(jnp.int32, w.shape, 1)
        w = jnp.where(lane < valid, w, 0.0)
    o_ref[...] = w.astype(BF16)


def _split_piece(w_in_t, lo, width, block, name):
    d = D_MODEL
    n_blocks = -(-width // block)
    in_block = (pl.Element(1), pl.Element(block), pl.Element(d))
    return pl.pallas_call(
        functools.partial(_split_piece_body, valid=min(width, block)),
        grid=(DEPTH, n_blocks),
        in_specs=[pl.BlockSpec(in_block, lambda l, i: (l, pl.multiple_of(lo + block * i, 16), 0))],
        out_specs=pl.BlockSpec((None, d, block), lambda l, i: (l, 0, i)),
        out_shape=jax.ShapeDtypeStruct((DEPTH, d, n_blocks * block), BF16),
        compiler_params=pltpu.CompilerParams(vmem_limit_bytes=VMEM_LIMIT),
        name=name,
    )(w_in_t)


def _split_w_in(w_in):
    w_in_t = jnp.swapaxes(w_in, 1, 2)
    pieces = {name: _split_piece(w_in_t, lo, hi - lo, W_IN_SPLIT_BLOCK, "split_" + name)
              for name, lo, hi in W_IN_PIECES}
    pieces["wlr"] = _split_piece(w_in_t, OFF_LR, GLA_RANK, LANES, "split_wlr")
    return pieces


def _prompt_layer_body(x_ref, mod_ref, ng_ref, wgla_ref, wlr_ref, gwa_ref, gba_ref, gng_ref, gwb_ref,
                       wsc_ref, sccw_ref, scwb_ref, wrg_ref, rgcw_ref, rgcb_ref, rgwax_ref, rgba_ref,
                       rgbx_ref, rglam_ref, rgwb_ref, wm_ref, bm_ref, wout_ref, fg_ref,
                       y_ref, gla_ref, sct_ref, rgt_ref, rgh_ref,
                       st_ref, hcar_ref, slab_h_ref, slab_y_ref, *, final):
    d = D_MODEL
    i = pl.program_id(1)

    @pl.when(i == 0)
    def _():
        st_ref[...] = jnp.zeros(st_ref.shape, F32)
        sct_ref[...] = jnp.zeros(sct_ref.shape, F32)
        rgt_ref[...] = jnp.zeros(rgt_ref.shape, F32)
        hcar_ref[...] = jnp.zeros(hcar_ref.shape, F32)

    x = x_ref[...]
    mod = mod_ref[...]
    gate = mod[:, 2 * d:3 * d]
    h = _mod_norm(x, mod, ng_ref[...])
    hb = h.astype(BF16)
    hb_s = _to_segment_major(h, slab_h_ref).astype(BF16)

    y_rg = _from_segment_major(
        _prompt_rg_branch(hb_s, wrg_ref, rgcw_ref, rgcb_ref, rgwax_ref, rgba_ref, rgbx_ref, rglam_ref,
                          rgwb_ref, hcar_ref, rgt_ref, rgh_ref), slab_y_ref)
    t_gla = _prompt_gla_branch(hb, wgla_ref, wlr_ref, gwa_ref, gba_ref, gng_ref, st_ref)
    y_gla = None
    for hd in range(GLA_HEADS):
        y_gla = _acc(y_gla, _mm(t_gla[hd], gwb_ref[hd * GLA_DV:(hd + 1) * GLA_DV, :]))
    y_sc = _from_segment_major(
        _mm(_prompt_sc_branch(hb_s, wsc_ref, sccw_ref, sct_ref), scwb_ref[...]), slab_y_ref)
    logits = tuple(_mm(hb, wm_ref[:, n * d:(n + 1) * d]) for n in range(3))
    xn = _merge_out(x, gate, (y_gla, y_sc, y_rg), logits, bm_ref, wout_ref)
    if final:
        xn = _rms(xn, fg_ref[...])
    y_ref[...] = xn

    @pl.when(i == pl.num_programs(1) - 1)
    def _():
        for hd in range(GLA_HEADS):
            gla_ref[hd] = st_ref[hd].T


def _to_segment_major(v, slab_ref):
    for lb in range(D_MODEL // LANES):
        for j in range(SUBLANES):
            slab_ref[lb, j * SLAB_PITCH:j * SLAB_PITCH + SEG_STEPS, :] = (
                v[j * SEG_STEPS:(j + 1) * SEG_STEPS, lb * LANES:(lb + 1) * LANES])
    steps = []
    for s in range(SEG_STEPS):
        steps.append(jnp.concatenate(
            [slab_ref[lb, pl.ds(s, SUBLANES, stride=SLAB_PITCH), :] for lb in range(D_MODEL // LANES)], axis=1))
    return jnp.concatenate(steps, axis=0)


def _from_segment_major(v, slab_ref):
    for s in range(SEG_STEPS):
        for lb in range(D_MODEL // LANES):
            slab_ref[lb, pl.ds(s, SUBLANES, stride=SLAB_PITCH), :] = (
                v[s * SUBLANES:(s + 1) * SUBLANES, lb * LANES:(lb + 1) * LANES])
    cols = []
    for lb in range(D_MODEL // LANES):
        cols.append(jnp.concatenate(
            [slab_ref[lb, j * SLAB_PITCH:j * SLAB_PITCH + SEG_STEPS, :] for j in range(SUBLANES)], axis=0))
    return jnp.concatenate(cols, axis=1)


def _seg_conv(v, w, tails, bias=None):
    k = w.shape[0] - 1
    width = v.shape[1]
    sub = lax.broadcasted_iota(jnp.int32, (SUBLANES, width), 0)
    out = v * w[k:k + 1]
    if bias is not None:
        out = out + bias
    for back in range(1, k + 1):
        heads = []
        for r in range(back):
            s = SEG_STEPS - back + r
            wrapped = pltpu.roll(v[s * SUBLANES:(s + 1) * SUBLANES, :], 1, 0)
            heads.append(jnp.where(sub == 0, tails[k - back + r:k - back + r + 1, :], wrapped))
        shifted = jnp.concatenate(heads + [v[0:(SEG_STEPS - back) * SUBLANES, :]], axis=0)
        out = out + shifted * w[k - back:k - back + 1]
    return out


def _seg_tails(v, k):
    return jnp.concatenate(
        [v[(SEG_STEPS - q) * SUBLANES + SUBLANES - 1:(SEG_STEPS - q + 1) * SUBLANES, :] for q in range(k, 0, -1)],
        axis=0)


def _seg_scan(a, b, h0):
    def step(s, v):
        return v[s * SUBLANES:(s + 1) * SUBLANES, :]
    prod, loc = step(0, a), step(0, b)
    for s in range(1, SEG_STEPS):
        loc = step(s, a) * loc + step(s, b)
        prod = prod * step(s, a)
    h_in = h0
    entering = []
    for j in range(SUBLANES):
        entering.append(h_in)
        h_in = prod[j:j + 1] * h_in + loc[j:j + 1]
    cur = jnp.concatenate(entering, axis=0)
    outs = []
    for s in range(SEG_STEPS):
        cur = step(s, a) * cur + step(s, b)
        outs.append(cur)
    return jnp.concatenate(outs, axis=0), h_in


def _prompt_gla_branch(hb, wgla_ref, wlr_ref, gwa_ref, gba_ref, gng_ref, st_ref):
    tt = ROW_TILE
    zlr = _mm(hb, wlr_ref[...])
    zq = _mm(hb, wgla_ref[:, 0:HDK]) * (GLA_DK ** -0.5)
    pre = _mm(zlr, gwa_ref[...]) + gba_ref[...]
    zk = _mm(hb, wgla_ref[:, HDK:2 * HDK])
    la = jax.nn.log_sigmoid(pre) * (LOG2_E / GLA_GATE_NORM)
    mb = GLA_MASK_BLOCK
    row_blocks = [slice(r * mb, (r + 1) * mb) for r in range(tt // mb)]
    row = lax.broadcasted_iota(jnp.int32, (mb, mb), 0)
    col = lax.broadcasted_iota(jnp.int32, (mb, mb), 1)
    chunk_bits = GLA_CHUNK.bit_length() - 1
    causal = jnp.logical_and(jnp.right_shift(row, chunk_bits) == jnp.right_shift(col, chunk_bits), col <= row)
    tri = jnp.where(causal, 1.0, 0.0).astype(BF16)
    la_hi = la.astype(BF16)
    la_lo = (la - la_hi.astype(F32)).astype(BF16)
    bcum = jnp.concatenate(
        [jnp.dot(tri, la_hi[rb], preferred_element_type=F32) + jnp.dot(tri, la_lo[rb], preferred_element_type=F32)
         for rb in row_blocks], axis=0)
    zv = _mm(hb, wgla_ref[:, 2 * HDK:2 * HDK + HDV])
    zg = _mm(hb, wgla_ref[:, 2 * HDK + HDV:2 * HDK + 2 * HDV])
    gng = gng_ref[...]
    n_chunk = tt // GLA_CHUNK
    gated = []
    for h in range(GLA_HEADS):
        ks = slice(h * GLA_DK, (h + 1) * GLA_DK)
        vs = slice(h * GLA_DV, (h + 1) * GLA_DV)
        b = bcum[:, ks]
        b3 = b.reshape(n_chunk, GLA_CHUNK, GLA_DK)
        b_mid = jnp.broadcast_to(b3[:, GLA_HALF - 1:GLA_HALF, :], b3.shape).reshape(tt, GLA_DK)
        b_end = jnp.broadcast_to(b3[:, GLA_CHUNK - 1:GLA_CHUNK, :], b3.shape).reshape(tt, GLA_DK)
        q = zq[:, ks]
        k = zk[:, ks]
        qe = q * jnp.exp2(b - b_mid)
        ke = k * jnp.exp2(b_mid - b)
        qs = (q * jnp.exp2(b)).astype(BF16)
        kd = (k * jnp.exp2(b_end - b)).astype(BF16)
        vb = zv[:, vs].astype(BF16)
        o = jnp.concatenate(
            [jnp.dot(jnp.where(causal, _mm_nt(qe[rb], ke[rb]), 0.0).astype(BF16), vb[rb],
                     preferred_element_type=F32) for rb in row_blocks], axis=0)
        st = st_ref[h]
        o_parts = []
        for c in range(n_chunk):
            rs = slice(c * GLA_CHUNK, (c + 1) * GLA_CHUNK)
            o_parts.append(o[rs] + _mm_nt(qs[rs], st))
            e_end = jnp.exp2(b3[c, GLA_CHUNK - 1:GLA_CHUNK, :])
            st = st * e_end + _mm_tn(vb[rs], kd[rs])
        st_ref[h] = st
        gated.append(_rms(jnp.concatenate(o_parts, axis=0), gng) * _silu(zg[:, vs]))
    return gated


def _prompt_sc_branch(hb, wsc_ref, sccw_ref, sct_ref):
    d = D_MODEL
    u = _mm(hb, wsc_ref[:, d:2 * d]) * _mm(hb, wsc_ref[:, 2 * d:3 * d])
    uc = _seg_conv(u, sccw_ref[...], sct_ref[...])
    sct_ref[...] = _seg_tails(u, SC_CONV_W - 1)
    zb = _mm(hb, wsc_ref[:, 0:d])
    zgs = _mm(hb, wsc_ref[:, 3 * d:4 * d])
    return zb * uc * _silu(zgs)


def _prompt_rg_branch(hb, wrg_ref, rgcw_ref, rgcb_ref, rgwax_ref, rgba_ref, rgbx_ref, rglam_ref, rgwb_ref,
                      hcar_ref, rgt_ref, rgh_ref):
    d = D_MODEL
    n_cb = d // COL_BLOCK
    rw = rgcw_ref[...]
    old_tails = rgt_ref[...]

    def project(cb):
        return _mm(hb, wrg_ref[:, _col(cb)]), _mm(hb, wrg_ref[:, _col(cb, d)])

    def conv_and_gate_logits(cb, zx):
        cs = _col(cb)
        xc = _seg_conv(zx, rw[:, cs], old_tails[:, cs], rgcb_ref[:, cs])
        rgt_ref[:, cs] = _seg_tails(zx, RG_CONV_W - 1)
        return (xc,) + _rg_gate_logits(xc, cb, rgwax_ref)

    proj = {0: project(0), 1: project(1)}
    logits = {0: conv_and_gate_logits(0, proj[0][0])}
    coeffs = []
    for cb in range(n_cb):
        if cb + 2 < n_cb:
            proj[cb + 2] = project(cb + 2)
        if cb + 1 < n_cb:
            logits[cb + 1] = conv_and_gate_logits(cb + 1, proj[cb + 1][0])
        coeffs.append(_rg_coeffs(*logits[cb], cb, rgba_ref, rgbx_ref, rglam_ref))
    a = jnp.concatenate([c[0] for c in coeffs], axis=1)
    bt = jnp.concatenate([c[1] for c in coeffs], axis=1)
    h_all, h_last = _seg_scan(a, bt, hcar_ref[...])
    hcar_ref[...] = h_last
    rgh_ref[...] = h_last
    zgr = jnp.concatenate([proj[cb][1] for cb in range(n_cb)], axis=1)
    return _mm(h_all * _silu(zgr), rgwb_ref[...])


def _layer_spec(stacked, layer):
    rest = stacked.shape[1:]
    return pl.BlockSpec((None,) + rest, lambda *_: (layer,) + (0,) * len(rest), pipeline_mode=pl.Buffered(1))


def _whole_spec(arr):
    return pl.BlockSpec(arr.shape, lambda *_: (0,) * arr.ndim, pipeline_mode=pl.Buffered(1))


PROMPT_WEIGHTS = ("ng", "wgla", "wlr", "gwa", "gba", "gng", "gwb", "wsc", "sccw", "scwb", "wrg", "rgcw",
                  "rgcb", "rgwax", "rgba", "rgbx", "rglam", "rgwb", "wm", "bm", "wout")


def _prompt_layer(x, mod_all, mod_row0, p, final_gain, layer, final):
    bsz, seq, d = x.shape
    tt = ROW_TILE
    weights = tuple(p[name] for name in PROMPT_WEIGHTS)
    mod4 = mod_all.reshape(DEPTH, mod_all.shape[1], 1, 3 * d)
    in_specs = [pl.BlockSpec((None, tt, d), lambda b, i: (b, i, 0)),
                pl.BlockSpec((None, None, 1, 3 * d), lambda b, i: (layer, mod_row0 + b, 0, 0))]
    in_specs += [_layer_spec(w, layer) for w in weights] + [_whole_spec(final_gain)]
    out_specs = [pl.BlockSpec((None, tt, d), lambda b, i: (b, i, 0)),
                 pl.BlockSpec((None, GLA_HEADS, GLA_DK, GLA_DV), lambda b, i: (b, 0, 0, 0)),
                 pl.BlockSpec((None, SC_CONV_W - 1, d), lambda b, i: (b, 0, 0)),
                 pl.BlockSpec((None, RG_CONV_W - 1, d), lambda b, i: (b, 0, 0)),
                 pl.BlockSpec((None, 1, d), lambda b, i: (b, 0, 0))]
    out_shape = [jax.ShapeDtypeStruct((bsz, seq, d), F32),
                 jax.ShapeDtypeStruct((bsz, GLA_HEADS, GLA_DK, GLA_DV), F32),
                 jax.ShapeDtypeStruct((bsz, SC_CONV_W - 1, d), F32),
                 jax.ShapeDtypeStruct((bsz, RG_CONV_W - 1, d), F32),
                 jax.ShapeDtypeStruct((bsz, 1, d), F32)]
    scratch = [pltpu.VMEM((GLA_HEADS, GLA_DV, GLA_DK), F32),
               pltpu.VMEM((1, d), F32)]
    scratch += [pltpu.VMEM((d // LANES, SUBLANES * SLAB_PITCH, LANES), F32)] * 2
    return pl.pallas_call(
        functools.partial(_prompt_layer_body, final=final),
        grid=(bsz, seq // tt),
        in_specs=in_specs,
        out_specs=out_specs,
        out_shape=out_shape,
        scratch_shapes=scratch,
        compiler_params=pltpu.CompilerParams(
            dimension_semantics=("arbitrary", "arbitrary"), vmem_limit_bytes=VMEM_LIMIT),
        name="prompt_layer",
    )(x, mod4, *weights, final_gain)


def _sample_gla_prep_body(x_ref, mod_ref, ng_ref, wgla_ref, wlr_ref, gwa_ref, gba_ref,
                          q_ref, k_ref, v_ref, ea_ref):
    hb = _mod_norm(x_ref[...], mod_ref[...], ng_ref[...]).astype(BF16)
    la = _gla_log_decay(hb, wlr_ref, gwa_ref, gba_ref)
    ea_ref[...] = jnp.exp(la)
    q_ref[...] = _mm(hb, wgla_ref[:, 0:HDK]) * (GLA_DK ** -0.5)
    k_ref[...] = _mm(hb, wgla_ref[:, HDK:2 * HDK])
    v_ref[...] = _mm(hb, wgla_ref[:, 2 * HDK:2 * HDK + HDV])


def _sample_mod_spec(rows, layer):
    return pl.BlockSpec((None, rows, 3 * D_MODEL), lambda *_: (layer, 0, 0))


def _sample_gla_prep(x, mod_all, p, layer):
    rows = x.shape[0]
    weights = tuple(p[name] for name in ("ng", "wgla", "wlr", "gwa", "gba"))
    out_shape = [jax.ShapeDtypeStruct((rows, HDK), F32),
                 jax.ShapeDtypeStruct((rows, HDK), F32),
                 jax.ShapeDtypeStruct((rows, HDV), F32),
                 jax.ShapeDtypeStruct((rows, HDK), F32)]
    return pl.pallas_call(
        _sample_gla_prep_body,
        grid=(1,),
        in_specs=[_whole_spec(x), _sample_mod_spec(rows, layer)] + [_layer_spec(w, layer) for w in weights],
        out_specs=[pl.BlockSpec(s.shape, lambda i: (0, 0)) for s in out_shape],
        out_shape=out_shape,
        compiler_params=pltpu.CompilerParams(vmem_limit_bytes=VMEM_LIMIT),
        name="sample_gla_prep",
    )(x, mod_all, *weights)


def _sample_gla_state_body(q_ref, k_ref, ea_ref, v_ref, s_ref, *rest):
    o_ref, so_ref = rest[-2:]
    for h in range(GLA_HEADS):
        ks = slice(h * GLA_DK, (h + 1) * GLA_DK)
        vs = slice(h * GLA_DV, (h + 1) * GLA_DV)
        q_t = q_ref[:, ks].T
        k_t = k_ref[:, ks].T
        e_t = ea_ref[:, ks].T
        for j in range(STATE_BATCH_TILE):
            s_new = e_t[:, j:j + 1] * s_ref[j, h] + k_t[:, j:j + 1] * v_ref[j:j + 1, vs]
            so_ref[j, h] = s_new
            o_ref[j:j + 1, vs] = jnp.sum(q_t[:, j:j + 1] * s_new, axis=0, keepdims=True)


def _sample_gla_state(q, k, ea, v, state, layer, carrier):
    rows = q.shape[0]
    bt = STATE_BATCH_TILE
    state_block = (None, bt, GLA_HEADS, GLA_DK, GLA_DV)
    state_index = lambda i: (layer, i, 0, 0, 0)
    args = [q, k, ea, v, state]
    in_specs = [pl.BlockSpec((bt, HDK), lambda i: (i, 0)),
                pl.BlockSpec((bt, HDK), lambda i: (i, 0)),
                pl.BlockSpec((bt, HDK), lambda i: (i, 0)),
                pl.BlockSpec((bt, HDV), lambda i: (i, 0)),
                pl.BlockSpec(state_block, state_index)]
    aliases = {}
    if carrier is not None:
        args.append(carrier)
        in_specs.append(pl.BlockSpec(memory_space=pl.ANY))
        aliases = {len(args) - 1: 1}
    return pl.pallas_call(
        _sample_gla_state_body,
        grid=(rows // bt,),
        in_specs=in_specs,
        out_specs=[pl.BlockSpec((bt, HDV), lambda i: (i, 0)),
                   pl.BlockSpec(state_block, state_index)],
        out_shape=[jax.ShapeDtypeStruct((rows, HDV), F32),
                   jax.ShapeDtypeStruct(state.shape, F32)],
        input_output_aliases=aliases,
        compiler_params=pltpu.CompilerParams(vmem_limit_bytes=VMEM_LIMIT),
        name="sample_gla_state",
    )(*args)


def _sample_tail_body(x_ref, mod_ref, o_ref, ssc_ref, srg_ref, srh_ref,
                      ng_ref, wgla_ref, gng_ref, gwb_ref, wsc_ref, sccw_ref, scwb_ref, wrg_ref, rgcw_ref,
                      rgcb_ref, rgwax_ref, rgba_ref, rgbx_ref, rglam_ref, rgwb_ref, wm_ref, bm_ref, wout_ref,
                      fg_ref, y_ref, nsc_ref, nrg_ref, nrh_ref, *, final):
    d = D_MODEL
    x = x_ref[...]
    mod = mod_ref[...]
    gate = mod[:, 2 * d:3 * d]
    hb = _mod_norm(x, mod, ng_ref[...]).astype(BF16)

    zg = _mm(hb, wgla_ref[:, 2 * HDK + HDV:2 * HDK + 2 * HDV])
    parts = _gla_out_norm_gate(o_ref[...], zg, gng_ref[...])
    y_gla = _mm(jnp.concatenate(parts, axis=1), gwb_ref[...])

    u = _mm(hb, wsc_ref[:, d:2 * d]) * _mm(hb, wsc_ref[:, 2 * d:3 * d])
    cw = sccw_ref[...]
    s0, s1 = ssc_ref[:, 0:d], ssc_ref[:, d:2 * d]
    uc = s0 * cw[0:1] + s1 * cw[1:2] + u * cw[2:3]
    nsc_ref[:, 0:d] = s1
    nsc_ref[:, d:2 * d] = u
    zb = _mm(hb, wsc_ref[:, 0:d])
    zgs = _mm(hb, wsc_ref[:, 3 * d:4 * d])
    y_sc = _mm(zb * uc * _silu(zgs), scwb_ref[...])

    zx = _mm(hb, wrg_ref[:, 0:d])
    rw = rgcw_ref[...]
    r0, r1, r2 = srg_ref[:, 0:d], srg_ref[:, d:2 * d], srg_ref[:, 2 * d:3 * d]
    xc = r0 * rw[0:1] + r1 * rw[1:2] + r2 * rw[2:3] + zx * rw[3:4] + rgcb_ref[...]
    nrg_ref[:, 0:d] = r1
    nrg_ref[:, d:2 * d] = r2
    nrg_ref[:, 2 * d:3 * d] = zx
    h_parts = []
    for cb in range(d // COL_BLOCK):
        cs = _col(cb)
        r_pre, i_pre = _rg_gate_logits(xc[:, cs], cb, rgwax_ref)
        a, bt = _rg_coeffs(xc[:, cs], r_pre, i_pre, cb, rgba_ref, rgbx_ref, rglam_ref)
        h_parts.append(a * srh_ref[:, cs] + bt)
    hr = jnp.concatenate(h_parts, axis=1)
    nrh_ref[...] = hr
    zgr = _mm(hb, wrg_ref[:, d:2 * d])
    y_rg = _mm(hr * _silu(zgr), rgwb_ref[...])

    logits = tuple(_mm(hb, wm_ref[:, n * d:(n + 1) * d]) for n in range(3))
    xn = _merge_out(x, gate, (y_gla, y_sc, y_rg), logits, bm_ref, wout_ref)
    if final:
        xn = _rms(xn, fg_ref[...])
    y_ref[...] = xn


def _sample_tail(x, mod_all, o, ssc, srg, srh, p, final_gain, layer, final):
    rows, d = x.shape
    weights = tuple(p[name] for name in ("ng", "wgla", "gng", "gwb", "wsc", "sccw", "scwb", "wrg", "rgcw",
                                         "rgcb", "rgwax", "rgba", "rgbx", "rglam", "rgwb", "wm", "bm", "wout"))
    out_shape = [jax.ShapeDtypeStruct((rows, d), F32),
                 jax.ShapeDtypeStruct((rows, (SC_CONV_W - 1) * d), F32),
                 jax.ShapeDtypeStruct((rows, (RG_CONV_W - 1) * d), F32),
                 jax.ShapeDtypeStruct((rows, d), F32)]
    in_specs = [_whole_spec(x), _sample_mod_spec(rows, layer), _whole_spec(o),
                _layer_spec(ssc, layer), _layer_spec(srg, layer), _layer_spec(srh, layer)]
    in_specs += [_layer_spec(w, layer) for w in weights] + [_whole_spec(final_gain)]
    return pl.pallas_call(
        functools.partial(_sample_tail_body, final=final),
        grid=(1,),
        in_specs=in_specs,
        out_specs=[pl.BlockSpec(s.shape, lambda i: (0, 0)) for s in out_shape],
        out_shape=out_shape,
        compiler_params=pltpu.CompilerParams(vmem_limit_bytes=VMEM_LIMIT),
        name="sample_tail",
    )(x, mod_all, o, ssc, srg, srh, *weights, final_gain)


def _stacked_params(norm_gain, w_in, gla_w_alpha, gla_b_alpha, gla_norm_gain, gla_w_branch, sc_conv_w,
                    sc_w_branch, rg_conv_w, rg_conv_b, rg_w_a, rg_b_a, rg_w_x, rg_b_x, rg_lambda,
                    rg_w_branch, b_merge, w_out):
    d = D_MODEL
    return {
        **_split_w_in(w_in),
        "ng": norm_gain.reshape(DEPTH, 1, d),
        "gwa": jnp.pad(gla_w_alpha.astype(BF16), ((0, 0), (0, LANES - GLA_RANK), (0, 0))),
        "gba": gla_b_alpha.reshape(DEPTH, 1, HDK),
        "gng": gla_norm_gain.reshape(DEPTH, 1, GLA_DV),
        "gwb": gla_w_branch.astype(BF16),
        "sccw": sc_conv_w,
        "scwb": sc_w_branch.astype(BF16),
        "rgcw": rg_conv_w,
        "rgcb": rg_conv_b.reshape(DEPTH, 1, d),
        "rgwax": jnp.concatenate([rg_w_a, rg_w_x], axis=-1).astype(BF16),
        "rgba": rg_b_a.reshape(DEPTH, 1, d),
        "rgbx": rg_b_x.reshape(DEPTH, 1, d),
        "rglam": rg_lambda.reshape(DEPTH, 1, d),
        "rgwb": rg_w_branch.astype(BF16),
        "bm": b_merge.reshape(DEPTH, 1, 3 * d),
        "wout": w_out.astype(BF16),
    }


def kernel(x_prompt, x_sample, c_prompt, c_sample, state_gla, state_sc_conv, state_rg_conv, state_rg_h, w_ada, b_ada, norm_gain, w_in, gla_w_alpha, gla_b_alpha, gla_norm_gain, gla_w_branch, sc_conv_w, sc_w_branch, rg_conv_w, rg_conv_b, rg_w_a, rg_b_a, rg_w_x, rg_b_x, rg_lambda, rg_w_branch, b_merge, w_out, final_gain):
    assert DEPTH == 2
    d = D_MODEL
    bp = x_prompt.shape[0]
    bs = x_sample.shape[0]
    fg = final_gain.reshape(1, d)
    mod_all = _ada_mod(jnp.concatenate([c_sample, c_prompt], axis=0), w_ada, b_ada)
    p = _stacked_params(norm_gain, w_in, gla_w_alpha, gla_b_alpha, gla_norm_gain, gla_w_branch,
                        sc_conv_w, sc_w_branch, rg_conv_w, rg_conv_b, rg_w_a, rg_b_a, rg_w_x, rg_b_x,
                        rg_lambda, rg_w_branch, b_merge, w_out)
    ssc = state_sc_conv.reshape(DEPTH, bs, (SC_CONV_W - 1) * d)
    srg = state_rg_conv.reshape(DEPTH, bs, (RG_CONV_W - 1) * d)

    xp = x_prompt
    xs = x_sample.reshape(bs, d)
    gla_p, sc_p, rgc_p, rgh_p = [], [], [], []
    gla_s, sc_s, rgc_s, rgh_s = None, [], [], []
    for l in range(DEPTH):
        final = l == DEPTH - 1
        xp, g, sc, rc, rh = _prompt_layer(xp, mod_all, bs, p, fg, l, final)
        gla_p.append(g)
        sc_p.append(sc)
        rgc_p.append(rc)
        rgh_p.append(rh.reshape(bp, d))

        q, k, v, ea = _sample_gla_prep(xs, mod_all, p, l)
        o, gla_s = _sample_gla_state(q, k, ea, v, state_gla, l, gla_s)
        xs, nsc, nrg, nrh = _sample_tail(xs, mod_all, o, ssc, srg, state_rg_h, p, fg, l, final)
        sc_s.append(nsc.reshape(bs, SC_CONV_W - 1, d))
        rgc_s.append(nrg.reshape(bs, RG_CONV_W - 1, d))
        rgh_s.append(nrh)

    return (xp, xs.reshape(bs, 1, d),
            jnp.stack(gla_p), jnp.stack(sc_p), jnp.stack(rgc_p), jnp.stack(rgh_p),
            gla_s, jnp.stack(sc_s), jnp.stack(rgc_s), jnp.stack(rgh_s))
```
